```python
import jax, jax.numpy as jnp
from jax import lax
import numpy as np

D_MODEL = 1024
BATCH = 8
SEQ = 8192
DEPTH = 1
DEC_BATCH = 8
DEC_SEQ = 32
PAST_LEN = 1024

CHUNK = 64
DSA_HEADS = 8
FOX_HEADS = 8
HEAD_DIM = D_MODEL // (DSA_HEADS + FOX_HEADS)
MIX_WIDTH = (DSA_HEADS + FOX_HEADS) * HEAD_DIM
DSA_KV_HEADS = 2
IDX_HEADS = 4
IDX_DIM = 64
TOPK_MAX = 256
ROPE_THETA = 500000.0
ROT_DIM = HEAD_DIM // 4
FORGET_BIAS_INIT = 2.0
N_MEM = 256
MEM_HEADS = 4
MEM_HEAD_DIM = D_MODEL // MEM_HEADS
PEER_HEADS = 8
PEER_KEY_DIM = 256
N_KEYS = 128
N_EXPERTS = N_KEYS * N_KEYS
PEER_TOPK = 16
Q_BLOCK = 128
TOK_BLOCK = 128
EPS = 1e-6
IN_SPLITS = (DSA_HEADS * HEAD_DIM, DSA_KV_HEADS * HEAD_DIM, DSA_KV_HEADS * HEAD_DIM, IDX_HEADS * IDX_DIM, IDX_DIM, IDX_HEADS, FOX_HEADS * HEAD_DIM, FOX_HEADS * HEAD_DIM, FOX_HEADS * HEAD_DIM, FOX_HEADS)
IN_WIDTH = int(sum(IN_SPLITS))
SPLIT_POINTS = tuple(int(c) for c in np.cumsum(IN_SPLITS)[:-1])

kernel_name = 'hybrid_dsa_fox_peer_stream_step'


def rms_norm(x, g):
    xf = x.astype(jnp.float32)
    y = xf * lax.rsqrt(jnp.mean(xf * xf, axis=-1, keepdims=True) + EPS)
    return (y * g.astype(jnp.float32)).astype(x.dtype)


def rope_partial(x, pos):
    inv_freq = ROPE_THETA ** (-jnp.arange(0, ROT_DIM, 2, dtype=jnp.float32) / ROT_DIM)
    ang = pos.astype(jnp.float32)[:, None] * inv_freq[None, :]
    ang = jnp.concatenate([ang, ang], axis=-1)
    shape = (1, pos.shape[0]) + (1,) * (x.ndim - 3) + (ROT_DIM,)
    cos = jnp.cos(ang).reshape(shape)
    sin = jnp.sin(ang).reshape(shape)
    xr = x[..., :ROT_DIM].astype(jnp.float32)
    x1, x2 = xr[..., :ROT_DIM // 2], xr[..., ROT_DIM // 2:]
    rot = jnp.concatenate([-x2, x1], axis=-1)
    xr = xr * cos + rot * sin
    return jnp.concatenate([xr.astype(x.dtype), x[..., ROT_DIM:]], axis=-1)


def sweep_queries(fn, q_tensors, q_pos):
    tq = q_pos.shape[0]
    if tq <= Q_BLOCK or tq % Q_BLOCK != 0:
        return fn(*q_tensors, q_pos)
    n = tq // Q_BLOCK
    blocks = tuple(jnp.moveaxis(t.reshape((t.shape[0], n, Q_BLOCK) + t.shape[2:]), 1, 0) for t in q_tensors)
    pos_b = q_pos.reshape(n, Q_BLOCK)
    out = lax.map(lambda a: fn(*a[0], a[1]), (blocks, pos_b))
    out = jnp.moveaxis(out, 0, 1)
    return out.reshape((out.shape[0], tq) + out.shape[3:])


def dsa_attention(q, qi, wi, k_all, v_all, ki_all, q_pos, k_pos, topk):
    def blk(q_b, qi_b, wi_b, pos_b):
        b, tb = q_b.shape[:2]
        idx = jax.nn.relu(jnp.einsum('bthd,bsd->bths', qi_b, ki_all))
        score = jnp.einsum('bths,bth->bts', idx, wi_b).astype(jnp.float32)
        admissible = (k_pos[None, :] // CHUNK) <= (pos_b[:, None] // CHUNK)
        score = jnp.where(admissible[None], score, -jnp.inf)
        sel_score, sel = lax.top_k(score, topk)
        valid = jnp.isfinite(sel_score)
        kg = jax.vmap(lambda kb, ib: kb[ib])(k_all, sel)
        vg = jax.vmap(lambda vb, ib: vb[ib])(v_all, sel)
        qg = q_b.reshape(b, tb, DSA_KV_HEADS, DSA_HEADS // DSA_KV_HEADS, HEAD_DIM)
        logits = jnp.einsum('btgrd,btkgd->btgrk', qg, kg).astype(jnp.float32) * (HEAD_DIM ** -0.5)
        logits = jnp.where(valid[:, :, None, None, :], logits, -jnp.inf)
        p = jax.nn.softmax(logits, axis=-1).astype(vg.dtype)
        o = jnp.einsum('btgrk,btkgd->btgrd', p, vg)
        return o.reshape(b, tb, DSA_HEADS * HEAD_DIM)
    return sweep_queries(blk, (q, qi, wi), q_pos)


def fox_attention(q, k_all, v_all, dq, dk_all, q_pos, k_pos):
    dk_t = jnp.swapaxes(dk_all, 1, 2)
    def blk(q_b, dq_b, pos_b):
        b, tb = q_b.shape[:2]
        logits = jnp.einsum('bthd,bshd->bhts', q_b, k_all).astype(jnp.float32) * (HEAD_DIM ** -0.5)
        logits = logits + jnp.swapaxes(dq_b, 1, 2)[:, :, :, None] - dk_t[:, :, None, :]
        mask = k_pos[None, :] <= pos_b[:, None]
        logits = jnp.where(mask[None, None], logits, -jnp.inf)
        p = jax.nn.softmax(logits, axis=-1).astype(v_all.dtype)
        o = jnp.einsum('bhts,bshd->bthd', p, v_all)
        return o.reshape(b, tb, FOX_HEADS * HEAD_DIM)
    return sweep_queries(blk, (q, dq), q_pos)


def memory_kv(mem, g, w_kv):
    b, m, _ = mem.shape
    kv = rms_norm(mem, g) @ w_kv
    k = kv[..., :D_MODEL].reshape(b, m, MEM_HEADS, MEM_HEAD_DIM)
    v = kv[..., D_MODEL:].reshape(b, m, MEM_HEADS, MEM_HEAD_DIM)
    return k, v


def memory_attention(xn, w_q, mk, mv, w_o):
    b, t, _ = xn.shape
    q = (xn @ w_q).reshape(b, t, MEM_HEADS, MEM_HEAD_DIM)
    logits = jnp.einsum('bthd,bmhd->bhtm', q, mk).astype(jnp.float32) * (MEM_HEAD_DIM ** -0.5)
    p = jax.nn.softmax(logits, axis=-1).astype(mv.dtype)
    o = jnp.einsum('bhtm,bmhd->bthd', p, mv).reshape(b, t, D_MODEL)
    return o @ w_o


def peer_ffn(xn, w_q, keys1, keys2, u, v):
    b, t, d = xn.shape
    n_tok = b * t
    pad = (-n_tok) % TOK_BLOCK
    tok = jnp.pad(xn.reshape(n_tok, d), ((0, pad), (0, 0)))
    blocks = tok.reshape(-1, TOK_BLOCK, d)
    half = PEER_KEY_DIM // 2
    def blk(xb):
        q = (xb @ w_q).reshape(TOK_BLOCK, PEER_HEADS, PEER_KEY_DIM)
        s1 = jnp.einsum('thd,hnd->thn', q[..., :half], keys1).astype(jnp.float32)
        s2 = jnp.einsum('thd,hnd->thn', q[..., half:], keys2).astype(jnp.float32)
        v1, i1 = lax.top_k(s1, PEER_TOPK)
        v2, i2 = lax.top_k(s2, PEER_TOPK)
        cand = (v1[..., :, None] + v2[..., None, :]).reshape(TOK_BLOCK, PEER_HEADS, PEER_TOPK * PEER_TOPK)
        cidx = (i1[..., :, None] * N_KEYS + i2[..., None, :]).reshape(TOK_BLOCK, PEER_HEADS, PEER_TOPK * PEER_TOPK)
        sv, si = lax.top_k(cand, PEER_TOPK)
        eidx = jnp.take_along_axis(cidx, si, axis=-1)
        g = jax.nn.softmax(sv, axis=-1)
        h = jax.nn.gelu(jnp.einsum('td,thkd->thk', xb, u[eidx]).astype(jnp.float32), approximate=False)
        return jnp.einsum('thk,thkd->td', (g * h).astype(v.dtype), v[eidx])
    out = lax.map(blk, blocks).reshape(-1, d)[:n_tok]
    return out.reshape(b, t, d)


def trunk_layer(x, pos, past, past_pos, mk, mv, norm_mix_g, w_in, b_fox_f, w_out, norm_mem_g, w_mem_q, w_mem_o, norm_ffn_g, peer_w_q, peer_keys1, peer_keys2, peer_u, peer_v):
    b, t, _ = x.shape
    xn = rms_norm(x, norm_mix_g)
    a_q, a_k, a_v, a_qi, a_ki, a_w, f_q, f_k, f_v, f_f = jnp.split(xn @ w_in, SPLIT_POINTS, axis=-1)
    a_q = rope_partial(a_q.reshape(b, t, DSA_HEADS, HEAD_DIM), pos)
    a_k = rope_partial(a_k.reshape(b, t, DSA_KV_HEADS, HEAD_DIM), pos)
    a_v = a_v.reshape(b, t, DSA_KV_HEADS, HEAD_DIM)
    a_qi = rope_partial(a_qi.reshape(b, t, IDX_HEADS, IDX_DIM), pos)
    a_ki = rope_partial(a_ki, pos)
    f_q = f_q.reshape(b, t, FOX_HEADS, HEAD_DIM)
    f_k = f_k.reshape(b, t, FOX_HEADS, HEAD_DIM)
    f_v = f_v.reshape(b, t, FOX_HEADS, HEAD_DIM)
    logf = jax.nn.log_sigmoid(f_f.astype(jnp.float32) + b_fox_f.astype(jnp.float32))
    new_rows = (a_k, a_v, a_ki, f_k, f_v, logf)
    if past is None:
        ak_all, av_all, aki_all, fk_all, fv_all, logf_all = new_rows
        k_pos = pos
    else:
        ak_all, av_all, aki_all, fk_all, fv_all, logf_all = tuple(jnp.concatenate([p, n], axis=1) for p, n in zip(past, new_rows))
        k_pos = jnp.concatenate([past_pos, pos], axis=0)
    n_keys_visible = ak_all.shape[1]
    topk = min(TOPK_MAX, n_keys_visible // 4)
    dk_all = jnp.cumsum(logf_all.astype(jnp.float32), axis=1)
    dq = dk_all[:, n_keys_visible - t:]
    o_a = dsa_attention(a_q, a_qi, a_w, ak_all, av_all, aki_all, pos, k_pos, topk)
    o_f = fox_attention(f_q, fk_all, fv_all, dq, dk_all, pos, k_pos)
    h = x + jnp.concatenate([o_a, o_f], axis=-1) @ w_out
    h = h + memory_attention(rms_norm(h, norm_mem_g), w_mem_q, mk, mv, w_mem_o)
    h = h + peer_ffn(rms_norm(h, norm_ffn_g), peer_w_q, peer_keys1, peer_keys2, peer_u, peer_v)
    return h, new_rows


def setup_inputs(seed: int = 0) -> dict:
    key = jax.random.key(seed)
    ks = jax.random.split(key, 32)
    f32 = jnp.float32
    def nrm(k, shape, scale):
        return jax.random.normal(k, shape, f32) * scale
    def gain(k, shape):
        return 1.0 + 0.05 * jax.random.normal(k, shape, f32)
    return {
        'x_prompt': nrm(ks[0], (BATCH, SEQ, D_MODEL), 1.0),
        'x_sample': nrm(ks[1], (DEC_BATCH, DEC_SEQ, D_MODEL), 1.0),
        'cache_dsa_k': nrm(ks[2], (DEPTH, DEC_BATCH, PAST_LEN, DSA_KV_HEADS, HEAD_DIM), 1.0),
        'cache_dsa_v': nrm(ks[3], (DEPTH, DEC_BATCH, PAST_LEN, DSA_KV_HEADS, HEAD_DIM), 1.0),
        'cache_dsa_kidx': nrm(ks[4], (DEPTH, DEC_BATCH, PAST_LEN, IDX_DIM), 1.0),
        'cache_fox_k': nrm(ks[5], (DEPTH, DEC_BATCH, PAST_LEN, FOX_HEADS, HEAD_DIM), 1.0),
        'cache_fox_v': nrm(ks[6], (DEPTH, DEC_BATCH, PAST_LEN, FOX_HEADS, HEAD_DIM), 1.0),
        'cache_fox_logf': jax.nn.log_sigmoid(FORGET_BIAS_INIT + jax.random.normal(ks[7], (DEPTH, DEC_BATCH, PAST_LEN, FOX_HEADS), f32)),
        'cache_mem_k': nrm(ks[8], (DEPTH, DEC_BATCH, N_MEM, MEM_HEADS, MEM_HEAD_DIM), 1.0),
        'cache_mem_v': nrm(ks[9], (DEPTH, DEC_BATCH, N_MEM, MEM_HEADS, MEM_HEAD_DIM), 1.0),
        'mem_prompt': nrm(ks[10], (BATCH, N_MEM, D_MODEL), 1.0),
        'norm_mix_g': gain(ks[11], (DEPTH, D_MODEL)),
        'w_in': nrm(ks[12], (DEPTH, D_MODEL, IN_WIDTH), D_MODEL ** -0.5),
        'b_fox_f': FORGET_BIAS_INIT + 0.5 * jax.random.normal(ks[13], (DEPTH, FOX_HEADS), f32),
        'w_out': nrm(ks[14], (DEPTH, MIX_WIDTH, D_MODEL), MIX_WIDTH ** -0.5),
        'mem_norm_g': gain(ks[15], (DEPTH, D_MODEL)),
        'w_mem_kv': nrm(ks[16], (DEPTH, D_MODEL, 2 * D_MODEL), D_MODEL ** -0.5),
        'norm_mem_g': gain(ks[17], (DEPTH, D_MODEL)),
        'w_mem_q': nrm(ks[18], (DEPTH, D_MODEL, D_MODEL), D_MODEL ** -0.5),
        'w_mem_o': nrm(ks[19], (DEPTH, D_MODEL, D_MODEL), D_MODEL ** -0.5),
        'norm_ffn_g': gain(ks[20], (DEPTH, D_MODEL)),
        'peer_w_q': nrm(ks[21], (DEPTH, D_MODEL, PEER_HEADS * PEER_KEY_DIM), D_MODEL ** -0.5),
        'peer_keys1': nrm(ks[22], (DEPTH, PEER_HEADS, N_KEYS, PEER_KEY_DIM // 2), (PEER_KEY_DIM // 2) ** -0.5),
        'peer_keys2': nrm(ks[23], (DEPTH, PEER_HEADS, N_KEYS, PEER_KEY_DIM // 2), (PEER_KEY_DIM // 2) ** -0.5),
        'peer_u': nrm(ks[24], (DEPTH, N_EXPERTS, D_MODEL), D_MODEL ** -0.5),
        'peer_v': nrm(ks[25], (DEPTH, N_EXPERTS, D_MODEL), 0.5),
        'final_norm_g': gain(ks[26], (D_MODEL,)),
    }


def reference(x_prompt, x_sample, cache_dsa_k, cache_dsa_v, cache_dsa_kidx, cache_fox_k, cache_fox_v, cache_fox_logf, cache_mem_k, cache_mem_v, mem_prompt, norm_mix_g, w_in, b_fox_f, w_out, mem_norm_g, w_mem_kv, norm_mem_g, w_mem_q, w_mem_o, norm_ffn_g, peer_w_q, peer_keys1, peer_keys2, peer_u, peer_v, final_norm_g):
    t_p = x_prompt.shape[1]
    t_s = x_sample.shape[1]
    past_len = cache_dsa_k.shape[2]
    pos_p = jnp.arange(t_p, dtype=jnp.int32)
    pos_past = jnp.arange(past_len, dtype=jnp.int32)
    pos_s = past_len + jnp.arange(t_s, dtype=jnp.int32)
    h_p, h_s = x_prompt, x_sample
    rows_p, rows_s = [], []
    for l in range(DEPTH):
        weights = (norm_mix_g[l], w_in[l], b_fox_f[l], w_out[l], norm_mem_g[l], w_mem_q[l], w_mem_o[l], norm_ffn_g[l], peer_w_q[l], peer_keys1[l], peer_keys2[l], peer_u[l], peer_v[l])
        mk_p, mv_p = memory_kv(mem_prompt, mem_norm_g[l], w_mem_kv[l])
        h_p, new_p = trunk_layer(h_p, pos_p, None, None, mk_p, mv_p, *weights)
        past = (cache_dsa_k[l], cache_dsa_v[l], cache_dsa_kidx[l], cache_fox_k[l], cache_fox_v[l], cache_fox_logf[l])
        h_s, new_s = trunk_layer(h_s, pos_s, past, pos_past, cache_mem_k[l], cache_mem_v[l], *weights)
        rows_p.append(new_p + (mk_p, mv_p))
        rows_s.append(new_s)
    y_prompt = rms_norm(h_p, final_norm_g)
    y_sample = rms_norm(h_s, final_norm_g)
    st_p = [jnp.stack([r[i] for r in rows_p], axis=0) for i in range(8)]
    st_s = [jnp.stack([r[i] for r in rows_s], axis=0) for i in range(6)]
    p_dsa_k, p_dsa_v, p_dsa_kidx, p_fox_k, p_fox_v, p_fox_logf, p_mem_k, p_mem_v = st_p
    s_dsa_k, s_dsa_v, s_dsa_kidx, s_fox_k, s_fox_v, s_fox_logf = st_s
    return (y_prompt, y_sample, p_dsa_k, p_dsa_v, p_dsa_kidx, p_fox_k, p_fox_v, p_fox_logf, p_mem_k, p_mem_v, s_dsa_k, s_dsa_v, s_dsa_kidx, s_fox_k, s_fox_v, s_fox_logf)
```

```python
import functools

import jax
import jax.numpy as jnp
import numpy as np
from jax import lax
from jax.experimental import pallas as pl
from jax.experimental.pallas import tpu as pltpu

D_MODEL = 1024
CHUNK_SHIFT = 6
DSA_HEADS = 8
FOX_HEADS = 8
HEAD_DIM = 64
DSA_KV_HEADS = 2
IDX_HEADS = 4
IDX_DIM = 64
TOPK_MAX = 256
ROPE_THETA = 500000.0
ROT_DIM = HEAD_DIM // 4
N_MEM = 256
MEM_HEADS = 4
MEM_HEAD_DIM = D_MODEL // MEM_HEADS
PEER_HEADS = 8
PEER_KEY_DIM = 256
N_KEYS = 128
PEER_TOPK = 16
EPS = 1e-6
LANES = 128
VMEM_LIMIT = 48 * 1024 * 1024

F32 = jnp.float32
BF16 = jnp.bfloat16
I32 = jnp.int32

INT_MIN = int(np.iinfo(np.int32).min)
_NEG_INF_BITS = int(np.array(-np.inf, np.float32).view(np.int32))
KEY_NEG_INF = _NEG_INF_BITS ^ 0x7FFFFFFF
KEY_NEG_INF = KEY_NEG_INF - (1 << 32) if KEY_NEG_INF >= (1 << 31) else KEY_NEG_INF
MASK_VALUE = -1e30
MASK_BITS = int(np.array(MASK_VALUE, np.float32).view(np.int32))

SEG_Q = (0, 512)
SEG_K = (512, 640)
SEG_V = (640, 768)
SEG_QI = (768, 1024)
SEG_KI = (1024, 1152)
SEG_MISC = (1152, 1280)
SEG_FQ = (1280, 1792)
SEG_FK = (1792, 2304)
SEG_FV = (2304, 2816)
IN_COLS = 2816

CAND = [(a, b) for a in range(PEER_TOPK) for b in range(PEER_TOPK) if (a + 1) * (b + 1) <= PEER_TOPK]
N_CAND = len(CAND)


def _dot(a, b):
    return jnp.dot(a, b, preferred_element_type=F32)


def _dot_t(a, b):
    return lax.dot_general(a, b, (((1,), (1,)), ((), ())), preferred_element_type=F32)


def _rms(x, g):
    return x * lax.rsqrt(jnp.mean(x * x, axis=-1, keepdims=True) + EPS) * g


def _inproj_kernel(x_ref, g_ref, w_ref, b_ref, cos_ref, sa_ref, sb_ref,
                   aq_ref, ak_ref, av_ref, qi_ref, ki_ref, misc_ref, fq_ref, fk_ref, fv_ref,
                   akb_ref, avb_ref, kib_ref, fkb_ref, fvb_ref):
    xn = _rms(x_ref[...], g_ref[...]).astype(BF16)
    cos, sa, sb = cos_ref[...], sa_ref[...], sb_ref[...]

    def proj(seg):
        return _dot(xn, w_ref[:, seg[0]:seg[1]])

    def rope(y):
        outs = []
        for j in range(y.shape[1] // LANES):
            s = y[:, j * LANES:(j + 1) * LANES]
            outs.append(s * cos + pltpu.roll(s, LANES - ROT_DIM // 2, 1) * sa
                        + pltpu.roll(s, ROT_DIM // 2, 1) * sb)
        return outs[0] if len(outs) == 1 else jnp.concatenate(outs, axis=1)

    scale = HEAD_DIM ** -0.5
    aq_ref[...] = (rope(proj(SEG_Q)) * scale).astype(BF16)
    ak = rope(proj(SEG_K))
    ak_ref[...] = ak
    akb_ref[...] = ak.astype(BF16)
    av = proj(SEG_V)
    av_ref[...] = av
    avb_ref[...] = av.astype(BF16)
    qi_ref[...] = rope(proj(SEG_QI)).astype(BF16)
    ki = rope(proj(SEG_KI))
    ki_ref[...] = ki
    kib_ref[...] = ki.astype(BF16)
    z = proj(SEG_MISC)
    zb = z + b_ref[...]
    logf = jnp.minimum(zb, 0.0) - jnp.log1p(jnp.exp(-jnp.abs(zb)))
    lane = lax.broadcasted_iota(I32, z.shape, 1)
    misc_ref[...] = jnp.where((lane >= IDX_HEADS) & (lane < IDX_HEADS + FOX_HEADS), logf, z)
    fq_ref[...] = (proj(SEG_FQ) * scale).astype(BF16)
    fk = proj(SEG_FK)
    fk_ref[...] = fk
    fkb_ref[...] = fk.astype(BF16)
    fv = proj(SEG_FV)
    fv_ref[...] = fv
    fvb_ref[...] = fv.astype(BF16)


def _inproj(x2, g, w_cat, b_misc, cos, sa, sb, tb):
    n = x2.shape[0]
    ntab = cos.shape[0] // tb
    row = lambda w: pl.BlockSpec((tb, w), lambda i: (i, 0))
    tab = pl.BlockSpec((tb, LANES), lambda i: (i % ntab, 0))
    const = lambda s: pl.BlockSpec(s, lambda i: (0, 0))
    widths = [(512, BF16), (128, F32), (128, F32), (256, BF16), (128, F32), (128, F32),
              (512, BF16), (512, F32), (512, F32),
              (128, BF16), (128, BF16), (128, BF16), (512, BF16), (512, BF16)]
    return pl.pallas_call(
        _inproj_kernel,
        grid=(n // tb,),
        in_specs=[row(D_MODEL), const((1, D_MODEL)), const((D_MODEL, IN_COLS)), const((1, LANES)), tab, tab, tab],
        out_specs=[row(w) for w, _ in widths],
        out_shape=[jax.ShapeDtypeStruct((n, w), dt) for w, dt in widths],
        compiler_params=pltpu.CompilerParams(dimension_semantics=("arbitrary",), vmem_limit_bytes=VMEM_LIMIT),
        name="inproj",
    )(x2, g, w_cat, b_misc, cos, sa, sb)


def _split3(x):
    hi = x.astype(BF16)
    r1 = x - hi.astype(F32)
    mid = r1.astype(BF16)
    lo = (r1 - mid.astype(F32)).astype(BF16)
    return hi, mid, lo


def _cumsum_kernel(x_ref, tri_ref, o_ref):
    tri = tri_ref[...]

    def body(c, carry):
        hi, mid, lo = _split3(x_ref[0, c])
        out = _dot(hi, tri) + _dot(mid, tri) + _dot(lo, tri) + carry
        o_ref[0, c] = out
        return out[:, LANES - 1:LANES]

    lax.fori_loop(0, x_ref.shape[1], body, jnp.zeros((FOX_HEADS, 1), F32))


def _cumsum(logf_t, tri):
    b, nchunk = logf_t.shape[:2]
    spec = pl.BlockSpec((1, nchunk, FOX_HEADS, LANES), lambda i: (i, 0, 0, 0))
    return pl.pallas_call(
        _cumsum_kernel,
        grid=(b,),
        in_specs=[spec, pl.BlockSpec((LANES, LANES), lambda i: (0, 0))],
        out_specs=spec,
        out_shape=jax.ShapeDtypeStruct(logf_t.shape, F32),
        name="cumsum",
    )(logf_t, tri)


def _dsa_kernel(aq_ref, qi_ref, misc_ref, k_ref, v_ref, ki_ref, tri_ref, o_ref, s_ref,
                *, tq, kb, n_keys, q_off, topk):
    i = pl.program_id(1)
    sub = kb // LANES
    q_last = q_off + (i + 1) * tq - 1
    adm_len = jnp.minimum(n_keys, ((q_last >> CHUNK_SHIFT) + 1) << CHUNK_SHIFT)
    nkb = (adm_len + kb - 1) // kb
    qchunk = (q_off + i * tq + lax.broadcasted_iota(I32, (tq, 1), 0)) >> CHUNK_SHIFT
    lane = lax.broadcasted_iota(I32, (tq, LANES), 1)
    lo_half = lane < HEAD_DIM

    zero = jnp.zeros((), BF16)
    qh = []
    for j in range(IDX_HEADS // 2):
        slab = qi_ref[:, j * LANES:(j + 1) * LANES]
        qh += [jnp.where(lo_half, slab, zero), jnp.where(lo_half, zero, slab)]
    wi = [misc_ref[:, h:h + 1] for h in range(IDX_HEADS)]

    def p1(c, _):
        off = pl.multiple_of(c * kb, kb)
        kib = ki_ref[0, pl.ds(off, kb), :]
        sc = jnp.zeros((tq, kb), F32)
        for h in range(IDX_HEADS):
            sc = sc + jnp.maximum(_dot_t(qh[h], kib), 0.0) * wi[h]
        kpos = off + lax.broadcasted_iota(I32, (1, kb), 1)
        ok = ((kpos >> CHUNK_SHIFT) <= qchunk) & (kpos < n_keys)
        bits = lax.bitcast_convert_type(sc, I32)
        key = jnp.where(ok, bits ^ ((bits >> 31) & 0x7FFFFFFF), KEY_NEG_INF)
        for j in range(sub):
            s_ref[c * sub + j] = key[:, j * LANES:(j + 1) * LANES]
        return 0

    lax.fori_loop(0, nkb, p1, 0)

    def count_ge(trial):
        tb = jnp.broadcast_to(trial, (tq, LANES))

        def body(c, acc):
            for j in range(sub):
                acc = acc + (s_ref[c * sub + j] >= tb).astype(I32)
            return acc

        acc = lax.fori_loop(0, nkb, body, jnp.zeros((tq, LANES), I32))
        return jnp.sum(acc, axis=1, keepdims=True)

    def bisect(it, cand):
        trial = cand | (jnp.int32(1) << (31 - it))
        cnt = count_ge(trial ^ INT_MIN)
        return jnp.where(cnt >= topk, trial, cand)

    vstar = lax.fori_loop(0, 32, bisect, jnp.zeros((tq, 1), I32)) ^ INT_MIN
    need = (topk - count_ge(vstar + 1)).astype(F32)

    vb = jnp.broadcast_to(vstar, (tq, LANES))
    tri = tri_ref[...]

    def p2b(c, carry):
        for j in range(sub):
            blk = s_ref[c * sub + j]
            eq = blk == vb
            pref = _dot(jnp.where(eq, 1.0, 0.0).astype(BF16), tri)
            sel = ((blk > vb) | (eq & (pref + carry <= need))) & (blk > KEY_NEG_INF)
            s_ref[c * sub + j] = jnp.where(sel, 0, MASK_BITS)
            carry = carry + pref[:, LANES - 1:LANES]
        return carry

    lax.fori_loop(0, nkb, p2b, jnp.zeros((tq, 1), F32))

    rep = DSA_HEADS // DSA_KV_HEADS
    res = []
    for g in range(DSA_KV_HEADS):
        in_g = lo_half if g == 0 else jnp.logical_not(lo_half)
        q_all = jnp.concatenate(
            [jnp.where(in_g, aq_ref[:, r * LANES:(r + 1) * LANES], zero) for r in range(rep)], axis=0)

        def p3(c, carry):
            m, l, acc = carry
            off = pl.multiple_of(c * kb, kb)
            lg = _dot_t(q_all, k_ref[0, pl.ds(off, kb), :])
            bias = jnp.concatenate(
                [lax.bitcast_convert_type(s_ref[c * sub + j], F32) for j in range(sub)], axis=1)
            lg = lg + jnp.concatenate([bias] * rep, axis=0)
            mn = jnp.maximum(m, jnp.max(lg, axis=1, keepdims=True))
            a = jnp.exp(m - mn)
            p = jnp.exp(lg - mn)
            l = a * l + jnp.sum(p, axis=1, keepdims=True)
            acc = a * acc + _dot(p.astype(BF16), v_ref[0, pl.ds(off, kb), :])
            return mn, l, acc

        init = (jnp.full((rep * tq, 1), MASK_VALUE, F32), jnp.zeros((rep * tq, 1), F32),
                jnp.zeros((rep * tq, LANES), F32))
        _, l, acc = lax.fori_loop(0, nkb, p3, init)
        res.append(acc / l)
    for r in range(rep):
        o_ref[:, r * LANES:(r + 1) * LANES] = jnp.where(
            lo_half, res[0][r * tq:(r + 1) * tq], res[1][r * tq:(r + 1) * tq]).astype(BF16)


def _dsa(aq, qi, misc, k_b, v_b, ki_b, tri, *, batch, t, tq, kb, n_keys, q_off, topk):
    lp = k_b.shape[1]
    nq = t // tq
    qrow = lambda w: pl.BlockSpec((tq, w), lambda b, i: (b * nq + i, 0))
    keys = pl.BlockSpec((1, lp, LANES), lambda b, i: (b, 0, 0))
    kern = functools.partial(_dsa_kernel, tq=tq, kb=kb, n_keys=n_keys, q_off=q_off, topk=topk)
    return pl.pallas_call(
        kern,
        grid=(batch, nq),
        in_specs=[qrow(512), qrow(256), qrow(LANES), keys, keys, keys,
                  pl.BlockSpec((LANES, LANES), lambda b, i: (0, 0))],
        out_specs=qrow(512),
        out_shape=jax.ShapeDtypeStruct((batch * t, 512), BF16),
        scratch_shapes=[pltpu.VMEM((lp // LANES, tq, LANES), I32)],
        compiler_params=pltpu.CompilerParams(dimension_semantics=("arbitrary", "arbitrary"),
                                             vmem_limit_bytes=VMEM_LIMIT),
        name="dsa",
    )(aq, qi, misc, k_b, v_b, ki_b, tri)


def _fox_kernel(q_ref, k_ref, v_ref, dq_ref, dk_ref, o_ref, *, tq, kb, n_keys, q_off):
    i = pl.program_id(2)
    nkb = (jnp.minimum(n_keys, q_off + (i + 1) * tq) + kb - 1) // kb
    lane = lax.broadcasted_iota(I32, (tq, LANES), 1)
    lo_half = lane < HEAD_DIM
    zero = jnp.zeros((), BF16)
    q = q_ref[...]
    q_all = jnp.concatenate([jnp.where(lo_half, q, zero), jnp.where(lo_half, zero, q)], axis=0)
    dq = dq_ref[0, 0]
    qpos = q_off + i * tq + lax.broadcasted_iota(I32, (tq, 1), 0)

    def body(c, carry):
        off = pl.multiple_of(c * kb, kb)
        lg = _dot_t(q_all, k_ref[0, pl.ds(off, kb), :])
        vblk = v_ref[0, pl.ds(off, kb), :]
        dk = dk_ref[0, 0, c]
        kpos = off + lax.broadcasted_iota(I32, (1, kb), 1)
        ok = (kpos <= qpos) & (kpos < n_keys)
        out = []
        for e in range(2):
            m, l, acc = carry[e]
            lge = lg[e * tq:(e + 1) * tq] + dq[:, e:e + 1] - dk[e:e + 1, :]
            lge = jnp.where(ok, lge, MASK_VALUE)
            mn = jnp.maximum(m, jnp.max(lge, axis=1, keepdims=True))
            a = jnp.exp(m - mn)
            p = jnp.exp(lge - mn)
            l = a * l + jnp.sum(p, axis=1, keepdims=True)
            acc = a * acc + _dot(p.astype(BF16), vblk)
            out.append((mn, l, acc))
        return tuple(out)

    one = (jnp.full((tq, 1), MASK_VALUE, F32), jnp.zeros((tq, 1), F32), jnp.zeros((tq, LANES), F32))
    (_, l0, a0), (_, l1, a1) = lax.fori_loop(0, nkb, body, (one, one))
    o_ref[...] = jnp.where(lo_half, a0 / l0, a1 / l1).astype(BF16)


def _fox(fq, k_b, v_b, dq_p, dk_p, *, batch, t, tq, kb, n_keys, q_off):
    lp = k_b.shape[1]
    nq = t // tq
    npair = FOX_HEADS // 2
    kern = functools.partial(_fox_kernel, tq=tq, kb=kb, n_keys=n_keys, q_off=q_off)
    kv = pl.BlockSpec((1, lp, LANES), lambda b, p, i: (b, 0, p))
    qo = pl.BlockSpec((tq, LANES), lambda b, p, i: (b * nq + i, p))
    return pl.pallas_call(
        kern,
        grid=(batch, npair, nq),
        in_specs=[qo, kv, kv,
                  pl.BlockSpec((1, 1, tq, 2), lambda b, p, i: (b, p, i, 0)),
                  pl.BlockSpec((1, 1, lp // kb, 2, kb), lambda b, p, i: (b, p, 0, 0, 0))],
        out_specs=qo,
        out_shape=jax.ShapeDtypeStruct((batch * t, FOX_HEADS * HEAD_DIM), BF16),
        compiler_params=pltpu.CompilerParams(dimension_semantics=("arbitrary",) * 3,
                                             vmem_limit_bytes=VMEM_LIMIT),
        name="fox",
    )(fq, k_b, v_b, dq_p, dk_p)


def _memkv_kernel(x_ref, g_ref, w_ref, k_ref, v_ref, kb_ref, vb_ref):
    xn = _rms(x_ref[...], g_ref[...]).astype(BF16)
    k = _dot(xn, w_ref[:, :D_MODEL])
    v = _dot(xn, w_ref[:, D_MODEL:])
    k_ref[...] = k
    v_ref[...] = v
    kb_ref[...] = k.astype(BF16)
    vb_ref[...] = v.astype(BF16)


def _memkv(mem2, g, w_kv):
    n = mem2.shape[0]
    row = pl.BlockSpec((N_MEM, D_MODEL), lambda i: (i, 0))
    return pl.pallas_call(
        _memkv_kernel,
        grid=(n // N_MEM,),
        in_specs=[row, pl.BlockSpec((1, D_MODEL), lambda i: (0, 0)),
                  pl.BlockSpec((D_MODEL, 2 * D_MODEL), lambda i: (0, 0))],
        out_specs=[row] * 4,
        out_shape=[jax.ShapeDtypeStruct((n, D_MODEL), dt) for dt in (F32, F32, BF16, BF16)],
        compiler_params=pltpu.CompilerParams(dimension_semantics=("arbitrary",), vmem_limit_bytes=VMEM_LIMIT),
        name="memkv",
    )(mem2, g, w_kv)


def _post_kernel(x_ref, oa_ref, of_ref, woa_ref, wof_ref, gm_ref, wq_ref, mk_ref, mv_ref, wo_ref, gf_ref,
                 h_ref, xn_ref):
    h = x_ref[...] + _dot(oa_ref[...], woa_ref[...]) + _dot(of_ref[...], wof_ref[...])
    hn = _rms(h, gm_ref[...]).astype(BF16)
    q = (_dot(hn, wq_ref[...]) * (MEM_HEAD_DIM ** -0.5)).astype(BF16)
    outs = []
    for hd in range(MEM_HEADS):
        cs = slice(hd * MEM_HEAD_DIM, (hd + 1) * MEM_HEAD_DIM)
        lg = _dot_t(q[:, cs], mk_ref[0, :, cs])
        p = jnp.exp(lg - jnp.max(lg, axis=1, keepdims=True))
        p = p / jnp.sum(p, axis=1, keepdims=True)
        outs.append(_dot(p.astype(BF16), mv_ref[0, :, cs]))
    o = jnp.concatenate(outs, axis=1).astype(BF16)
    h = h + _dot(o, wo_ref[...])
    h_ref[...] = h
    xn_ref[...] = _rms(h, gf_ref[...])


def _post(x2, oa, of, woa, wof, gm, wq, mk_b, mv_b, wo, gf, *, batch, t, tb):
    nt = t // tb
    row = lambda w: pl.BlockSpec((tb, w), lambda b, i: (b * nt + i, 0))
    const = lambda s: pl.BlockSpec(s, lambda b, i: (0, 0))
    mem = pl.BlockSpec((1, N_MEM, D_MODEL), lambda b, i: (b, 0, 0))
    return pl.pallas_call(
        _post_kernel,
        grid=(batch, nt),
        in_specs=[row(D_MODEL), row(512), row(512), const((512, D_MODEL)), const((512, D_MODEL)),
                  const((1, D_MODEL)), const((D_MODEL, D_MODEL)), mem, mem, const((D_MODEL, D_MODEL)),
                  const((1, D_MODEL))],
        out_specs=[row(D_MODEL), row(D_MODEL)],
        out_shape=[jax.ShapeDtypeStruct((batch * t, D_MODEL), F32)] * 2,
        compiler_params=pltpu.CompilerParams(dimension_semantics=("arbitrary", "arbitrary"),
                                             vmem_limit_bytes=VMEM_LIMIT),
        name="post",
    )(x2, oa, of, woa, wof, gm, wq, mk_b, mv_b, wo, gf)


def _top16(s):
    r = s.shape[0]
    row = lax.broadcasted_iota(I32, s.shape, 0).astype(F32)
    vals, ids = [], []
    for _ in range(PEER_TOPK):
        m = jnp.max(s, axis=0, keepdims=True)
        am = jnp.min(jnp.where(s == m, row, float(r)), axis=0, keepdims=True)
        vals.append(m)
        ids.append(am)
        s = jnp.where(row == am, -jnp.inf, s)
    return jnp.concatenate(vals, axis=0), jnp.concatenate(ids, axis=0)


def _peer_sel_kernel(x_ref, wqt_ref, k1_ref, k2_ref, eidx_ref, gate_ref, qt_scr, e_scr, g_scr):
    half = PEER_KEY_DIM // 2
    qt_scr[...] = _dot_t(wqt_ref[...], x_ref[...].astype(BF16)).astype(BF16)
    n = qt_scr.shape[1]
    pad = (-N_CAND) % 8

    def head(h, _):
        r0 = pl.multiple_of(h * PEER_KEY_DIM, PEER_KEY_DIM)
        v1, i1 = _top16(_dot(k1_ref[h], qt_scr[pl.ds(r0, half), :]))
        v2, i2 = _top16(_dot(k2_ref[h], qt_scr[pl.ds(r0 + half, half), :]))
        cand = jnp.concatenate([v1[a:a + 1] + v2[b:b + 1] for a, b in CAND]
                               + [jnp.full((pad, n), -jnp.inf, F32)], axis=0)
        cidx = jnp.concatenate([i1[a:a + 1] * float(N_KEYS) + i2[b:b + 1] for a, b in CAND]
                               + [jnp.zeros((pad, n), F32)], axis=0)
        sv, sr = _top16(cand)
        row = lax.broadcasted_iota(I32, cand.shape, 0).astype(F32)
        eid = jnp.concatenate(
            [jnp.sum(jnp.where(row == sr[k:k + 1], cidx, 0.0), axis=0, keepdims=True) for k in range(PEER_TOPK)],
            axis=0)
        p = jnp.exp(sv - sv[0:1])
        o0 = pl.multiple_of(h * PEER_TOPK, PEER_TOPK)
        e_scr[pl.ds(o0, PEER_TOPK), :] = eid
        g_scr[pl.ds(o0, PEER_TOPK), :] = p / jnp.sum(p, axis=0, keepdims=True)
        return 0

    lax.fori_loop(0, PEER_HEADS, head, 0)
    eidx_ref[...] = e_scr[...].T.astype(I32)
    gate_ref[...] = g_scr[...].T


def _peer_sel(xn2, wq_t, keys1, keys2, tb):
    n = xn2.shape[0]
    kspec = pl.BlockSpec((PEER_HEADS, N_KEYS, PEER_KEY_DIM // 2), lambda i: (0, 0, 0))
    out = pl.BlockSpec((tb, PEER_HEADS * PEER_TOPK), lambda i: (i, 0))
    return pl.pallas_call(
        _peer_sel_kernel,
        grid=(n // tb,),
        in_specs=[pl.BlockSpec((tb, D_MODEL), lambda i: (i, 0)),
                  pl.BlockSpec((PEER_HEADS * PEER_KEY_DIM, D_MODEL), lambda i: (0, 0)), kspec, kspec],
        out_specs=[out, out],
        out_shape=[jax.ShapeDtypeStruct((n, PEER_HEADS * PEER_TOPK), I32),
                   jax.ShapeDtypeStruct((n, PEER_HEADS * PEER_TOPK), F32)],
        scratch_shapes=[pltpu.VMEM((PEER_HEADS * PEER_KEY_DIM, tb), BF16)]
        + [pltpu.VMEM((PEER_HEADS * PEER_TOPK, tb), F32)] * 2,
        compiler_params=pltpu.CompilerParams(dimension_semantics=("arbitrary",), vmem_limit_bytes=VMEM_LIMIT),
        name="peer_sel",
    )(xn2, wq_t, keys1, keys2)


N_PICK = PEER_HEADS * PEER_TOPK
N_SLOT = 3


def _erf(x):
    return lax.erf(x)


def _peer_ffn_kernel(eidx_ref, gate_ref, x_ref, h_ref, fg_ref, uv_ref, y_ref, buf, acc, sem, *, tt):
    def row_copy(t, j, slot):
        return pltpu.make_async_copy(uv_ref.at[pl.ds(eidx_ref[t, j], 1), :],
                                     buf.at[slot, pl.ds(j, 1), :], sem.at[slot])

    def issue(t):
        slot = t % N_SLOT

        def one(j, _):
            row_copy(t, j, slot).start()
            return 0

        lax.fori_loop(0, N_PICK, one, 0, unroll=8)

    for t0 in range(N_SLOT - 1):
        issue(t0)

    def token(t, _):
        @pl.when(t + N_SLOT - 1 < tt)
        def _():
            issue(t + N_SLOT - 1)

        slot = t % N_SLOT

        def wait_one(j, _):
            row_copy(t, j, slot).wait()
            return 0

        lax.fori_loop(0, N_PICK, wait_one, 0, unroll=8)
        rows = buf[slot]
        xb = jnp.broadcast_to(x_ref[pl.ds(t, 1), :].astype(BF16), (8, D_MODEL))
        hd = _dot_t(xb, rows[:, :D_MODEL].astype(BF16))
        w = gate_ref[pl.ds(t, 1), :] * (0.5 * hd * (1.0 + _erf(hd * (2.0 ** -0.5))))
        acc[pl.ds(t, 1), :] = _dot(w.astype(BF16), rows[:, D_MODEL:].astype(BF16))[0:1]
        return 0

    lax.fori_loop(0, tt, token, 0)
    y_ref[...] = _rms(h_ref[...] + acc[...], fg_ref[...])


def _peer_ffn(eidx, gate, xn2, h2, fg, uv, tt):
    n = xn2.shape[0]
    row = lambda w: pl.BlockSpec((tt, w), lambda i: (i, 0))
    return pl.pallas_call(
        functools.partial(_peer_ffn_kernel, tt=tt),
        grid=(n // tt,),
        in_specs=[pl.BlockSpec((tt, N_PICK), lambda i: (i, 0), memory_space=pltpu.SMEM),
                  row(N_PICK), row(D_MODEL), row(D_MODEL), pl.BlockSpec((1, D_MODEL), lambda i: (0, 0)),
                  pl.BlockSpec(memory_space=pl.ANY)],
        out_specs=row(D_MODEL),
        out_shape=jax.ShapeDtypeStruct((n, D_MODEL), F32),
        scratch_shapes=[pltpu.VMEM((N_SLOT, N_PICK, 2 * D_MODEL), F32), pltpu.VMEM((tt, D_MODEL), F32),
                        pltpu.SemaphoreType.DMA((N_SLOT,))],
        compiler_params=pltpu.CompilerParams(dimension_semantics=("arbitrary",), vmem_limit_bytes=VMEM_LIMIT),
        name="peer_ffn",
    )(eidx, gate, xn2, h2, fg, uv)


def _rope_tables(pos):
    inv_freq = ROPE_THETA ** (-jnp.arange(0, ROT_DIM, 2, dtype=F32) / ROT_DIM)
    ang = pos.astype(F32)[:, None] * inv_freq[None, :]
    cos, sin = jnp.cos(ang), jnp.sin(ang)
    n = pos.shape[0]
    rest = HEAD_DIM - ROT_DIM
    hr = ROT_DIM // 2
    c = jnp.concatenate([cos, cos, jnp.ones((n, rest), F32)], axis=1)
    a = jnp.concatenate([-sin, jnp.zeros((n, hr + rest), F32)], axis=1)
    b = jnp.concatenate([jnp.zeros((n, hr), F32), sin, jnp.zeros((n, rest), F32)], axis=1)
    return tuple(jnp.tile(z, (1, LANES // HEAD_DIM)) for z in (c, a, b))


Q_PERM = np.concatenate([np.r_[j * HEAD_DIM:(j + 1) * HEAD_DIM, (4 + j) * HEAD_DIM:(5 + j) * HEAD_DIM]
                         for j in range(4)])


def _prep_weights(norm_mix_g, w_in, b_fox_f, w_out, norm_mem_g, w_mem_q, w_mem_o, norm_ffn_g, peer_w_q,
                  peer_keys1, peer_keys2, peer_u, peer_v):
    sp = np.cumsum([0, 512, 128, 128, 256, 64, 4, 512, 512, 512, 8])
    a_q, a_k, a_v, a_qi, a_ki, a_w, f_q, f_k, f_v, f_f = [w_in[:, sp[i]:sp[i + 1]] for i in range(10)]
    zpad = jnp.zeros((D_MODEL, LANES - IDX_HEADS - FOX_HEADS), F32)
    w_cat = jnp.concatenate([a_q[:, Q_PERM], a_k, a_v, a_qi, a_ki, a_ki, a_w, f_f, zpad, f_q, f_k, f_v],
                            axis=1).astype(BF16)
    b_misc = jnp.concatenate([jnp.zeros((IDX_HEADS,), F32), b_fox_f.astype(F32),
                              jnp.zeros((LANES - IDX_HEADS - FOX_HEADS,), F32)])[None, :]
    r = lambda g: g.astype(F32)[None, :]
    return dict(
        g_mix=r(norm_mix_g), w_cat=w_cat, b_misc=b_misc,
        woa=w_out[:512][Q_PERM].astype(BF16), wof=w_out[512:].astype(BF16),
        g_mem=r(norm_mem_g), wq=w_mem_q.astype(BF16), wo=w_mem_o.astype(BF16), g_ffn=r(norm_ffn_g),
        wq_t=peer_w_q.T.astype(BF16), keys1=peer_keys1.astype(BF16), keys2=peer_keys2.astype(BF16),
        uv=jnp.concatenate([peer_u, peer_v], axis=1),
    )


def _round_up(x, m):
    return (x + m - 1) // m * m


def _layer(x, q_off, past, mk_b, mv_b, w, fg, tri, cfg):
    batch, t, _ = x.shape
    n = batch * t
    x2 = x.reshape(n, D_MODEL)
    tabs = _rope_tables(q_off + jnp.arange(t, dtype=I32))
    tb_in = cfg["tb_in"]
    if tb_in > t:
        tabs = tuple(jnp.tile(z, (tb_in // t, 1)) for z in tabs)
    (aq, ak, av, qi, ki, misc, fq, fk, fv, akb, avb, kib, fkb, fvb) = _inproj(
        x2, w["g_mix"], w["w_cat"], w["b_misc"], *tabs, tb_in)
    logf = misc[:, IDX_HEADS:IDX_HEADS + FOX_HEADS].reshape(batch, t, FOX_HEADS)
    new_rows = (ak.reshape(batch, t, DSA_KV_HEADS, HEAD_DIM), av.reshape(batch, t, DSA_KV_HEADS, HEAD_DIM),
                ki[:, :IDX_DIM].reshape(batch, t, IDX_DIM), fk.reshape(batch, t, FOX_HEADS, HEAD_DIM),
                fv.reshape(batch, t, FOX_HEADS, HEAD_DIM), logf)

    kb = cfg["kb"]
    n_past = 0 if past is None else past[0].shape[1]
    n_keys = n_past + t
    lp = _round_up(n_keys, kb)

    def keys(new_b, old, width):
        new_b = new_b.reshape(batch, t, width)
        parts = [new_b] if old is None else [old.reshape(batch, n_past, -1).astype(BF16), new_b]
        if lp > n_keys:
            parts.append(jnp.zeros((batch, lp - n_keys, width), BF16))
        return parts[0] if len(parts) == 1 else jnp.concatenate(parts, axis=1)

    if past is None:
        k_all, v_all, ki_all = keys(akb, None, LANES), keys(avb, None, LANES), keys(kib, None, LANES)
        fk_all, fv_all = keys(fkb, None, 512), keys(fvb, None, 512)
        logf_all = logf
    else:
        p_k, p_v, p_ki, p_fk, p_fv, p_logf = past
        ki_dup = jnp.concatenate([p_ki, p_ki], axis=-1)
        k_all, v_all, ki_all = keys(akb, p_k, LANES), keys(avb, p_v, LANES), keys(kib, ki_dup, LANES)
        fk_all, fv_all = keys(fkb, p_fk, 512), keys(fvb, p_fv, 512)
        logf_all = jnp.concatenate([p_logf.astype(F32), logf], axis=1)

    lf = jnp.pad(logf_all, ((0, 0), (0, lp - n_keys), (0, 0)))
    lf_t = lf.reshape(batch, lp // LANES, LANES, FOX_HEADS).transpose(0, 1, 3, 2)
    d_t = _cumsum(lf_t, tri).transpose(0, 2, 1, 3).reshape(batch, FOX_HEADS, lp)
    npair = FOX_HEADS // 2
    dq_p = d_t[:, :, n_keys - t:n_keys].reshape(batch, npair, 2, t).transpose(0, 1, 3, 2)
    dk_p = d_t.reshape(batch, npair, 2, lp // kb, kb).transpose(0, 1, 3, 2, 4)

    topk = min(TOPK_MAX, n_keys // 4)
    oa = _dsa(aq, qi, misc, k_all, v_all, ki_all, tri, batch=batch, t=t, tq=cfg["tq_dsa"], kb=kb,
              n_keys=n_keys, q_off=q_off, topk=topk)
    of = _fox(fq, fk_all, fv_all, dq_p, dk_p, batch=batch, t=t, tq=cfg["tq_fox"], kb=kb,
              n_keys=n_keys, q_off=q_off)
    h2, xn2 = _post(x2, oa, of, w["woa"], w["wof"], w["g_mem"], w["wq"], mk_b, mv_b, w["wo"], w["g_ffn"],
                    batch=batch, t=t, tb=cfg["tb_post"])
    eidx, gate = _peer_sel(xn2, w["wq_t"], w["keys1"], w["keys2"], cfg["tb_sel"])
    y = _peer_ffn(eidx, gate, xn2, h2, fg, w["uv"], cfg["tt_ffn"])
    return y.reshape(batch, t, D_MODEL), new_rows


def _config(t, n_tokens, has_past):
    if has_past:
        return dict(tb_in=n_tokens, kb=384, tq_dsa=t, tq_fox=t, tb_post=t, tb_sel=n_tokens, tt_ffn=min(128, n_tokens))
    return dict(tb_in=min(512, t), kb=min(512, t), tq_dsa=min(128, t), tq_fox=min(256, t), tb_post=min(256, t),
                tb_sel=min(256, n_tokens), tt_ffn=min(128, n_tokens))


def kernel(x_prompt, x_sample, cache_dsa_k, cache_dsa_v, cache_dsa_kidx, cache_fox_k, cache_fox_v, cache_fox_logf, cache_mem_k, cache_mem_v, mem_prompt, norm_mix_g, w_in, b_fox_f, w_out, mem_norm_g, w_mem_kv, norm_mem_g, w_mem_q, w_mem_o, norm_ffn_g, peer_w_q, peer_keys1, peer_keys2, peer_u, peer_v, final_norm_g):
    depth = w_in.shape[0]
    assert depth == 1, "single-layer trunk"
    l = 0
    bp, tp, _ = x_prompt.shape
    bs, ts, _ = x_sample.shape
    past_len = cache_dsa_k.shape[2]
    w = _prep_weights(norm_mix_g[l], w_in[l], b_fox_f[l], w_out[l], norm_mem_g[l], w_mem_q[l], w_mem_o[l],
                      norm_ffn_g[l], peer_w_q[l], peer_keys1[l], peer_keys2[l], peer_u[l], peer_v[l])
    fg = final_norm_g.astype(F32)[None, :]
    ii = np.arange(LANES)
    tri = jnp.asarray(ii[:, None] <= ii[None, :], BF16)

    mk, mv, mkb, mvb = _memkv(mem_prompt.reshape(bp * N_MEM, D_MODEL), mem_norm_g[l].astype(F32)[None, :],
                              w_mem_kv[l].astype(BF16))
    y_p, new_p = _layer(x_prompt, 0, None, mkb.reshape(bp, N_MEM, D_MODEL), mvb.reshape(bp, N_MEM, D_MODEL),
                        w, fg, tri, _config(tp, bp * tp, False))
    past = (cache_dsa_k[l], cache_dsa_v[l], cache_dsa_kidx[l], cache_fox_k[l], cache_fox_v[l], cache_fox_logf[l])
    y_s, new_s = _layer(x_sample, past_len, past, cache_mem_k[l].reshape(bs, N_MEM, D_MODEL).astype(BF16),
                        cache_mem_v[l].reshape(bs, N_MEM, D_MODEL).astype(BF16), w, fg, tri,
                        _config(ts, bs * ts, True))
    st = lambda a: a[None]
    mem_shape = (bp, N_MEM, MEM_HEADS, MEM_HEAD_DIM)
    return (y_p, y_s) + tuple(st(a) for a in new_p) + (st(mk.reshape(mem_shape)), st(mv.reshape(mem_shape))) \
        + tuple(st(a) for a in new_s)
```

```python
import functools

import jax
import jax.numpy as jnp
import numpy as np
from jax import lax
from jax.experimental import pallas as pl
from jax.experimental.pallas import tpu as pltpu

D_MODEL = 1024
CHUNK_SHIFT = 6
DSA_HEADS = 8
FOX_HEADS = 8
HEAD_DIM = 64
DSA_KV_HEADS = 2
IDX_HEADS = 4
IDX_DIM = 64
TOPK_MAX = 256
ROPE_THETA = 500000.0
ROT_DIM = HEAD_DIM // 4
N_MEM = 256
MEM_HEADS = 4
MEM_HEAD_DIM = D_MODEL // MEM_HEADS
PEER_HEADS = 8
PEER_KEY_DIM = 256
N_KEYS = 128
PEER_TOPK = 16
EPS = 1e-6
LANES = 128
VMEM_LIMIT = 48 * 1024 * 1024

F32 = jnp.float32
BF16 = jnp.bfloat16
I32 = jnp.int32

INT_MIN = int(np.iinfo(np.int32).min)
_NEG_INF_BITS = int(np.array(-np.inf, np.float32).view(np.int32))
KEY_NEG_INF = _NEG_INF_BITS ^ 0x7FFFFFFF
KEY_NEG_INF = KEY_NEG_INF - (1 << 32) if KEY_NEG_INF >= (1 << 31) else KEY_NEG_INF
MASK_VALUE = -1e30
MASK_BITS = int(np.array(MASK_VALUE, np.float32).view(np.int32))

SEG_Q = (0, 512)
SEG_K = (512, 640)
SEG_V = (640, 768)
SEG_QI = (768, 1024)
SEG_KI = (1024, 1152)
SEG_MISC = (1152, 1280)
SEG_FQ = (1280, 1792)
SEG_FK = (1792, 2304)
SEG_FV = (2304, 2816)
IN_COLS = 2816

CAND = [(a, b) for a in range(PEER_TOPK) for b in range(PEER_TOPK) if (a + 1) * (b + 1) <= PEER_TOPK]
N_CAND = len(CAND)


def _dot(a, b):
    return jnp.dot(a, b, preferred_element_type=F32)


def _dot_t(a, b):
    return lax.dot_general(a, b, (((1,), (1,)), ((), ())), preferred_element_type=F32)


def _rms(x, g):
    return x * lax.rsqrt(jnp.mean(x * x, axis=-1, keepdims=True) + EPS) * g


def _inproj_kernel(x_ref, g_ref, w_ref, b_ref, cos_ref, sa_ref, sb_ref,
                   aq_ref, ak_ref, av_ref, qi_ref, ki_ref, misc_ref, fq_ref, fk_ref, fv_ref,
                   akb_ref, avb_ref, kib_ref, fkb_ref, fvb_ref):
    xn = _rms(x_ref[...], g_ref[...]).astype(BF16)
    cos, sa, sb = cos_ref[...], sa_ref[...], sb_ref[...]

    def proj(seg):
        return _dot(xn, w_ref[:, seg[0]:seg[1]])

    def rope(y):
        outs = []
        for j in range(y.shape[1] // LANES):
            s = y[:, j * LANES:(j + 1) * LANES]
            outs.append(s * cos + pltpu.roll(s, LANES - ROT_DIM // 2, 1) * sa
                        + pltpu.roll(s, ROT_DIM // 2, 1) * sb)
        return outs[0] if len(outs) == 1 else jnp.concatenate(outs, axis=1)

    scale = HEAD_DIM ** -0.5
    aq_ref[...] = (rope(proj(SEG_Q)) * scale).astype(BF16)
    ak = rope(proj(SEG_K))
    ak_ref[...] = ak
    akb_ref[...] = ak.astype(BF16)
    av = proj(SEG_V)
    av_ref[...] = av
    avb_ref[...] = av.astype(BF16)
    qi_ref[...] = rope(proj(SEG_QI)).astype(BF16)
    ki = rope(proj(SEG_KI))
    ki_ref[...] = ki
    kib_ref[...] = ki.astype(BF16)
    z = proj(SEG_MISC)
    zb = z + b_ref[...]
    logf = jnp.minimum(zb, 0.0) - jnp.log1p(jnp.exp(-jnp.abs(zb)))
    lane = lax.broadcasted_iota(I32, z.shape, 1)
    misc_ref[...] = jnp.where((lane >= IDX_HEADS) & (lane < IDX_HEADS + FOX_HEADS), logf, z)
    fq_ref[...] = (proj(SEG_FQ) * scale).astype(BF16)
    fk = proj(SEG_FK)
    fk_ref[...] = fk
    fkb_ref[...] = fk.astype(BF16)
    fv = proj(SEG_FV)
    fv_ref[...] = fv
    fvb_ref[...] = fv.astype(BF16)


def _inproj(x2, g, w_cat, b_misc, cos, sa, sb, tb):
    n = x2.shape[0]
    ntab = cos.shape[0] // tb
    row = lambda w: pl.BlockSpec((tb, w), lambda i: (i, 0))
    tab = pl.BlockSpec((tb, LANES), lambda i: (i % ntab, 0))
    const = lambda s: pl.BlockSpec(s, lambda i: (0, 0))
    widths = [(512, BF16), (128, F32), (128, F32), (256, BF16), (128, F32), (128, F32),
              (512, BF16), (512, F32), (512, F32),
              (128, BF16), (128, BF16), (128, BF16), (512, BF16), (512, BF16)]
    return pl.pallas_call(
        _inproj_kernel,
        grid=(n // tb,),
        in_specs=[row(D_MODEL), const((1, D_MODEL)), const((D_MODEL, IN_COLS)), const((1, LANES)), tab, tab, tab],
        out_specs=[row(w) for w, _ in widths],
        out_shape=[jax.ShapeDtypeStruct((n, w), dt) for w, dt in widths],
        compiler_params=pltpu.CompilerParams(dimension_semantics=("arbitrary",), vmem_limit_bytes=VMEM_LIMIT),
        name="inproj",
    )(x2, g, w_cat, b_misc, cos, sa, sb)


def _split3(x):
    hi = x.astype(BF16)
    r1 = x - hi.astype(F32)
    mid = r1.astype(BF16)
    lo = (r1 - mid.astype(F32)).astype(BF16)
    return hi, mid, lo


def _cumsum_kernel(x_ref, tri_ref, o_ref):
    tri = tri_ref[...]

    def body(c, carry):
        hi, mid, lo = _split3(x_ref[0, c])
        out = _dot(hi, tri) + _dot(mid, tri) + _dot(lo, tri) + carry
        o_ref[0, c] = out
        return out[:, LANES - 1:LANES]

    lax.fori_loop(0, x_ref.shape[1], body, jnp.zeros((FOX_HEADS, 1), F32))


def _cumsum(logf_t, tri):
    b, nchunk = logf_t.shape[:2]
    spec = pl.BlockSpec((1, nchunk, FOX_HEADS, LANES), lambda i: (i, 0, 0, 0))
    return pl.pallas_call(
        _cumsum_kernel,
        grid=(b,),
        in_specs=[spec, pl.BlockSpec((LANES, LANES), lambda i: (0, 0))],
        out_specs=spec,
        out_shape=jax.ShapeDtypeStruct(logf_t.shape, F32),
        name="cumsum",
    )(logf_t, tri)


def _dsa_kernel(aq_ref, qi_ref, misc_ref, k_ref, v_ref, ki_ref, tri_ref, o_ref, s_ref,
                *, tq, kb, n_keys, q_off, topk):
    i = pl.program_id(1)
    sub = kb // LANES
    q_last = q_off + (i + 1) * tq - 1
    adm_len = jnp.minimum(n_keys, ((q_last >> CHUNK_SHIFT) + 1) << CHUNK_SHIFT)
    nkb = (adm_len + kb - 1) // kb
    qchunk = (q_off + i * tq + lax.broadcasted_iota(I32, (tq, 1), 0)) >> CHUNK_SHIFT
    lane = lax.broadcasted_iota(I32, (tq, LANES), 1)
    lo_half = lane < HEAD_DIM

    zero = jnp.zeros((), BF16)
    qh = []
    for j in range(IDX_HEADS // 2):
        slab = qi_ref[:, j * LANES:(j + 1) * LANES]
        qh += [jnp.where(lo_half, slab, zero), jnp.where(lo_half, zero, slab)]
    wi = [misc_ref[:, h:h + 1] for h in range(IDX_HEADS)]

    def p1(c, _):
        off = pl.multiple_of(c * kb, kb)
        kib = ki_ref[0, pl.ds(off, kb), :]
        sc = jnp.zeros((tq, kb), F32)
        for h in range(IDX_HEADS):
            sc = sc + jnp.maximum(_dot_t(qh[h], kib), 0.0) * wi[h]
        kpos = off + lax.broadcasted_iota(I32, (1, kb), 1)
        ok = ((kpos >> CHUNK_SHIFT) <= qchunk) & (kpos < n_keys)
        bits = lax.bitcast_convert_type(sc, I32)
        key = jnp.where(ok, bits ^ ((bits >> 31) & 0x7FFFFFFF), KEY_NEG_INF)
        for j in range(sub):
            s_ref[c * sub + j] = key[:, j * LANES:(j + 1) * LANES]
        return 0

    lax.fori_loop(0, nkb, p1, 0)

    def count_ge(trial):
        tb = jnp.broadcast_to(trial, (tq, LANES))

        def body(c, acc):
            for j in range(sub):
                acc = acc + (s_ref[c * sub + j] >= tb).astype(I32)
            return acc

        acc = lax.fori_loop(0, nkb, body, jnp.zeros((tq, LANES), I32))
        return jnp.sum(acc, axis=1, keepdims=True)

    def bisect(it, cand):
        trial = cand | (jnp.int32(1) << (31 - it))
        cnt = count_ge(trial ^ INT_MIN)
        return jnp.where(cnt >= topk, trial, cand)

    vstar = lax.fori_loop(0, 32, bisect, jnp.zeros((tq, 1), I32)) ^ INT_MIN
    need = (topk - count_ge(vstar + 1)).astype(F32)

    vb = jnp.broadcast_to(vstar, (tq, LANES))
    tri = tri_ref[...]

    def p2b(c, carry):
        for j in range(sub):
            blk = s_ref[c * sub + j]
            eq = blk == vb
            pref = _dot(jnp.where(eq, 1.0, 0.0).astype(BF16), tri)
            sel = ((blk > vb) | (eq & (pref + carry <= need))) & (blk > KEY_NEG_INF)
            s_ref[c * sub + j] = jnp.where(sel, 0, MASK_BITS)
            carry = carry + pref[:, LANES - 1:LANES]
        return carry

    lax.fori_loop(0, nkb, p2b, jnp.zeros((tq, 1), F32))

    rep = DSA_HEADS // DSA_KV_HEADS
    res = []
    for g in range(DSA_KV_HEADS):
        in_g = lo_half if g == 0 else jnp.logical_not(lo_half)
        q_all = jnp.concatenate(
            [jnp.where(in_g, aq_ref[:, r * LANES:(r + 1) * LANES], zero) for r in range(rep)], axis=0)

        def p3(c, carry):
            m, l, acc = carry
            off = pl.multiple_of(c * kb, kb)
            lg = _dot_t(q_all, k_ref[0, pl.ds(off, kb), :])
            bias = jnp.concatenate(
                [lax.bitcast_convert_type(s_ref[c * sub + j], F32) for j in range(sub)], axis=1)
            lg = lg + jnp.concatenate([bias] * rep, axis=0)
            mn = jnp.maximum(m, jnp.max(lg, axis=1, keepdims=True))
            a = jnp.exp(m - mn)
            p = jnp.exp(lg - mn)
            l = a * l + jnp.sum(p, axis=1, keepdims=True)
            acc = a * acc + _dot(p.astype(BF16), v_ref[0, pl.ds(off, kb), :])
            return mn, l, acc

        init = (jnp.full((rep * tq, 1), MASK_VALUE, F32), jnp.zeros((rep * tq, 1), F32),
                jnp.zeros((rep * tq, LANES), F32))
        _, l, acc = lax.fori_loop(0, nkb, p3, init)
        res.append(acc / l)
    for r in range(rep):
        o_ref[:, r * LANES:(r + 1) * LANES] = jnp.where(
            lo_half, res[0][r * tq:(r + 1) * tq], res[1][r * tq:(r + 1) * tq]).astype(BF16)


def _dsa(aq, qi, misc, k_b, v_b, ki_b, tri, *, batch, t, tq, kb, n_keys, q_off, topk):
    lp = k_b.shape[1]
    nq = t // tq
    qrow = lambda w: pl.BlockSpec((tq, w), lambda b, i: (b * nq + i, 0))
    keys = pl.BlockSpec((1, lp, LANES), lambda b, i: (b, 0, 0))
    kern = functools.partial(_dsa_kernel, tq=tq, kb=kb, n_keys=n_keys, q_off=q_off, topk=topk)
    return pl.pallas_call(
        kern,
        grid=(batch, nq),
        in_specs=[qrow(512), qrow(256), qrow(LANES), keys, keys, keys,
                  pl.BlockSpec((LANES, LANES), lambda b, i: (0, 0))],
        out_specs=qrow(512),
        out_shape=jax.ShapeDtypeStruct((batch * t, 512), BF16),
        scratch_shapes=[pltpu.VMEM((lp // LANES, tq, LANES), I32)],
        compiler_params=pltpu.CompilerParams(dimension_semantics=("arbitrary", "arbitrary"),
                                             vmem_limit_bytes=VMEM_LIMIT),
        name="dsa",
    )(aq, qi, misc, k_b, v_b, ki_b, tri)


def _fox_kernel(q_ref, k_ref, v_ref, dq_ref, dk_ref, o_ref, *, tq, kb, n_keys, q_off):
    i = pl.program_id(2)
    nkb = (jnp.minimum(n_keys, q_off + (i + 1) * tq) + kb - 1) // kb
    lane = lax.broadcasted_iota(I32, (tq, LANES), 1)
    lo_half = lane < HEAD_DIM
    zero = jnp.zeros((), BF16)
    q = q_ref[...]
    q_all = jnp.concatenate([jnp.where(lo_half, q, zero), jnp.where(lo_half, zero, q)], axis=0)
    dq = dq_ref[0, 0]
    qpos = q_off + i * tq + lax.broadcasted_iota(I32, (tq, 1), 0)

    def body(c, carry):
        off = pl.multiple_of(c * kb, kb)
        lg = _dot_t(q_all, k_ref[0, pl.ds(off, kb), :])
        vblk = v_ref[0, pl.ds(off, kb), :]
        dk = dk_ref[0, 0, c]
        kpos = off + lax.broadcasted_iota(I32, (1, kb), 1)
        ok = (kpos <= qpos) & (kpos < n_keys)
        out = []
        for e in range(2):
            m, l, acc = carry[e]
            lge = lg[e * tq:(e + 1) * tq] + dq[:, e:e + 1] - dk[e:e + 1, :]
            lge = jnp.where(ok, lge, MASK_VALUE)
            mn = jnp.maximum(m, jnp.max(lge, axis=1, keepdims=True))
            a = jnp.exp(m - mn)
            p = jnp.exp(lge - mn)
            l = a * l + jnp.sum(p, axis=1, keepdims=True)
            acc = a * acc + _dot(p.astype(BF16), vblk)
            out.append((mn, l, acc))
        return tuple(out)

    one = (jnp.full((tq, 1), MASK_VALUE, F32), jnp.zeros((tq, 1), F32), jnp.zeros((tq, LANES), F32))
    (_, l0, a0), (_, l1, a1) = lax.fori_loop(0, nkb, body, (one, one))
    o_ref[...] = jnp.where(lo_half, a0 / l0, a1 / l1).astype(BF16)


def _fox(fq, k_b, v_b, dq_p, dk_p, *, batch, t, tq, kb, n_keys, q_off):
    lp = k_b.shape[1]
    nq = t // tq
    npair = FOX_HEADS // 2
    kern = functools.partial(_fox_kernel, tq=tq, kb=kb, n_keys=n_keys, q_off=q_off)
    kv = pl.BlockSpec((1, lp, LANES), lambda b, p, i: (b, 0, p))
    qo = pl.BlockSpec((tq, LANES), lambda b, p, i: (b * nq + i, p))
    return pl.pallas_call(
        kern,
        grid=(batch, npair, nq),
        in_specs=[qo, kv, kv,
                  pl.BlockSpec((1, 1, tq, 2), lambda b, p, i: (b, p, i, 0)),
                  pl.BlockSpec((1, 1, lp // kb, 2, kb), lambda b, p, i: (b, p, 0, 0, 0))],
        out_specs=qo,
        out_shape=jax.ShapeDtypeStruct((batch * t, FOX_HEADS * HEAD_DIM), BF16),
        compiler_params=pltpu.CompilerParams(dimension_semantics=("arbitrary",) * 3,
                                             vmem_limit_bytes=VMEM_LIMIT),
        name="fox",
    )(fq, k_b, v_b, dq_p, dk_p)


def _memkv_kernel(x_ref, g_ref, w_ref, k_ref, v_ref, kb_ref, vb_ref):
    xn = _rms(x_ref[...], g_ref[...]).astype(BF16)
    k = _dot(xn, w_ref[:, :D_MODEL])
    v = _dot(xn, w_ref[:, D_MODEL:])
    k_ref[...] = k
    v_ref[...] = v
    kb_ref[...] = k.astype(BF16)
    vb_ref[...] = v.astype(BF16)


def _memkv(mem2, g, w_kv):
    n = mem2.shape[0]
    row = pl.BlockSpec((N_MEM, D_MODEL), lambda i: (i, 0))
    return pl.pallas_call(
        _memkv_kernel,
        grid=(n // N_MEM,),
        in_specs=[row, pl.BlockSpec((1, D_MODEL), lambda i: (0, 0)),
                  pl.BlockSpec((D_MODEL, 2 * D_MODEL), lambda i: (0, 0))],
        out_specs=[row] * 4,
        out_shape=[jax.ShapeDtypeStruct((n, D_MODEL), dt) for dt in (F32, F32, BF16, BF16)],
        compiler_params=pltpu.CompilerParams(dimension_semantics=("arbitrary",), vmem_limit_bytes=VMEM_LIMIT),
        name="memkv",
    )(mem2, g, w_kv)


def _post_kernel(x_ref, oa_ref, of_ref, woa_ref, wof_ref, gm_ref, wq_ref, mk_ref, mv_ref, wo_ref, gf_ref,
                 h_ref, xn_ref):
    h = x_ref[...] + _dot(oa_ref[...], woa_ref[...]) + _dot(of_ref[...], wof_ref[...])
    hn = _rms(h, gm_ref[...]).astype(BF16)
    q = (_dot(hn, wq_ref[...]) * (MEM_HEAD_DIM ** -0.5)).astype(BF16)
    outs = []
    for hd in range(MEM_HEADS):
        cs = slice(hd * MEM_HEAD_DIM, (hd + 1) * MEM_HEAD_DIM)
        lg = _dot_t(q[:, cs], mk_ref[0, :, cs])
        p = jnp.exp(lg - jnp.max(lg, axis=1, keepdims=True))
        p = p / jnp.sum(p, axis=1, keepdims=True)
        outs.append(_dot(p.astype(BF16), mv_ref[0, :, cs]))
    o = jnp.concatenate(outs, axis=1).astype(BF16)
    h = h + _dot(o, wo_ref[...])
    h_ref[...] = h
    xn_ref[...] = _rms(h, gf_ref[...])


def _post(x2, oa, of, woa, wof, gm, wq, mk_b, mv_b, wo, gf, *, batch, t, tb):
    nt = t // tb
    row = lambda w: pl.BlockSpec((tb, w), lambda b, i: (b * nt + i, 0))
    const = lambda s: pl.BlockSpec(s, lambda b, i: (0, 0))
    mem = pl.BlockSpec((1, N_MEM, D_MODEL), lambda b, i: (b, 0, 0))
    return pl.pallas_call(
        _post_kernel,
        grid=(batch, nt),
        in_specs=[row(D_MODEL), row(512), row(512), const((512, D_MODEL)), const((512, D_MODEL)),
                  const((1, D_MODEL)), const((D_MODEL, D_MODEL)), mem, mem, const((D_MODEL, D_MODEL)),
                  const((1, D_MODEL))],
        out_specs=[row(D_MODEL), row(D_MODEL)],
        out_shape=[jax.ShapeDtypeStruct((batch * t, D_MODEL), F32)] * 2,
        compiler_params=pltpu.CompilerParams(dimension_semantics=("arbitrary", "arbitrary"),
                                             vmem_limit_bytes=VMEM_LIMIT),
        name="post",
    )(x2, oa, of, woa, wof, gm, wq, mk_b, mv_b, wo, gf)


def _top16(s):
    r = s.shape[0]
    row = lax.broadcasted_iota(I32, s.shape, 0).astype(F32)
    vals, ids = [], []
    for _ in range(PEER_TOPK):
        m = jnp.max(s, axis=0, keepdims=True)
        am = jnp.min(jnp.where(s == m, row, float(r)), axis=0, keepdims=True)
        vals.append(m)
        ids.append(am)
        s = jnp.where(row == am, -jnp.inf, s)
    return jnp.concatenate(vals, axis=0), jnp.concatenate(ids, axis=0)


def _peer_sel_kernel(x_ref, wqt_ref, k1_ref, k2_ref, eidx_ref, gate_ref, qt_scr, e_scr, g_scr):
    half = PEER_KEY_DIM // 2
    qt_scr[...] = _dot_t(wqt_ref[...], x_ref[...].astype(BF16)).astype(BF16)
    n = qt_scr.shape[1]
    pad = (-N_CAND) % 8

    def head(h, _):
        r0 = pl.multiple_of(h * PEER_KEY_DIM, PEER_KEY_DIM)
        v1, i1 = _top16(_dot(k1_ref[h], qt_scr[pl.ds(r0, half), :]))
        v2, i2 = _top16(_dot(k2_ref[h], qt_scr[pl.ds(r0 + half, half), :]))
        cand = jnp.concatenate([v1[a:a + 1] + v2[b:b + 1] for a, b in CAND]
                               + [jnp.full((pad, n), -jnp.inf, F32)], axis=0)
        cidx = jnp.concatenate([i1[a:a + 1] * float(N_KEYS) + i2[b:b + 1] for a, b in CAND]
                               + [jnp.zeros((pad, n), F32)], axis=0)
        sv, sr = _top16(cand)
        row = lax.broadcasted_iota(I32, cand.shape, 0).astype(F32)
        eid = jnp.concatenate(
            [jnp.sum(jnp.where(row == sr[k:k + 1], cidx, 0.0), axis=0, keepdims=True) for k in range(PEER_TOPK)],
            axis=0)
        p = jnp.exp(sv - sv[0:1])
        o0 = pl.multiple_of(h * PEER_TOPK, PEER_TOPK)
        e_scr[pl.ds(o0, PEER_TOPK), :] = eid
        g_scr[pl.ds(o0, PEER_TOPK), :] = p / jnp.sum(p, axis=0, keepdims=True)
        return 0

    lax.fori_loop(0, PEER_HEADS, head, 0)
    eidx_ref[...] = e_scr[...].T.astype(I32)
    gate_ref[...] = g_scr[...].T


def _peer_sel(xn2, wq_t, keys1, keys2, tb):
    n = xn2.shape[0]
    kspec = pl.BlockSpec((PEER_HEADS, N_KEYS, PEER_KEY_DIM // 2), lambda i: (0, 0, 0))
    out = pl.BlockSpec((tb, PEER_HEADS * PEER_TOPK), lambda i: (i, 0))
    return pl.pallas_call(
        _peer_sel_kernel,
        grid=(n // tb,),
        in_specs=[pl.BlockSpec((tb, D_MODEL), lambda i: (i, 0)),
                  pl.BlockSpec((PEER_HEADS * PEER_KEY_DIM, D_MODEL), lambda i: (0, 0)), kspec, kspec],
        out_specs=[out, out],
        out_shape=[jax.ShapeDtypeStruct((n, PEER_HEADS * PEER_TOPK), I32),
                   jax.ShapeDtypeStruct((n, PEER_HEADS * PEER_TOPK), F32)],
        scratch_shapes=[pltpu.VMEM((PEER_HEADS * PEER_KEY_DIM, tb), BF16)]
        + [pltpu.VMEM((PEER_HEADS * PEER_TOPK, tb), F32)] * 2,
        compiler_params=pltpu.CompilerParams(dimension_semantics=("arbitrary",), vmem_limit_bytes=VMEM_LIMIT),
        name="peer_sel",
    )(xn2, wq_t, keys1, keys2)


N_PICK = PEER_HEADS * PEER_TOPK
P_AHEAD = 2
N_SLOT = P_AHEAD + 3
GROUP = 8
HI_MASK = -65536


def _gelu(x):
    return 0.5 * x * (1.0 + lax.erf(x * (2.0 ** -0.5)))


def _peer_ffn_kernel(eidx_ref, gate_ref, x_ref, h_ref, fg_ref, uv_ref, y_ref, buf, hd_scr, wc_scr, acc, sem, *, tt):
    @pl.when(pl.program_id(0) == 0)
    def _():
        buf[...] = jnp.zeros(buf.shape, I32)
        hd_scr[...] = jnp.zeros(hd_scr.shape, F32)
        wc_scr[...] = jnp.zeros(wc_scr.shape, F32)

    def expert_copy(t, j):
        return pltpu.make_async_copy(uv_ref.at[eidx_ref[t * N_PICK + j]],
                                     buf.at[t % N_SLOT, :, j, :], sem.at[t % N_SLOT])

    def packed_rows(slot, rows):
        return jnp.concatenate([buf[slot, s, rows, :] for s in range(n_tiles)], axis=1)

    ones8 = jnp.ones((GROUP, LANES), BF16)
    ones_sq = jnp.ones((LANES, LANES), BF16)
    eye = lax.broadcasted_iota(I32, (LANES, LANES), 0) == lax.broadcasted_iota(I32, (LANES, LANES), 1)
    n_tiles = D_MODEL // LANES

    def lane_tiles_sum(a):
        out = a[:, 0:LANES]
        for c in range(1, n_tiles):
            out = out + a[:, c * LANES:(c + 1) * LANES]
        return out

    n_groups = N_PICK // GROUP
    mix_at = 4

    def step(t, issue, gather_u, mix_w):
        if gather_u:
            for _ in range(N_PICK):
                pltpu.make_async_copy(uv_ref.at[0], buf.at[t % N_SLOT, :, 0, :], sem.at[t % N_SLOT]).wait()
            xrow = jnp.broadcast_to(x_ref[pl.ds(t, 1), :], (GROUP, D_MODEL))
        slot_u = t % N_SLOT
        slot_v = (t + N_SLOT - 2) % N_SLOT
        acc8 = jnp.zeros((GROUP, D_MODEL), F32)
        part = None
        wcol_new = None
        for k in range(n_groups):
            rows = slice(k * GROUP, (k + 1) * GROUP)
            if issue:
                for r in range(GROUP):
                    expert_copy(t + P_AHEAD, k * GROUP + r).start(priority=r % 2)
            if gather_u:
                if part is not None:
                    hd_scr[t % 2, (k - 1) * GROUP:k * GROUP, :] = part
                u = lax.bitcast_convert_type(packed_rows(slot_u, rows) & HI_MASK, F32)
                part = lane_tiles_sum(u * xrow)
            v = lax.bitcast_convert_type(packed_rows(slot_v, rows) << 16, F32)
            acc8 = acc8 + v * jnp.concatenate([wc_scr[rows, :]] * n_tiles, axis=1)
            if mix_w and k == mix_at:
                hdp = hd_scr[(t + 1) % 2]
                hi = hdp.astype(BF16)
                lo = (hdp - hi.astype(F32)).astype(BF16)
                hd = _dot_t(ones8, hi) + _dot_t(ones8, lo)
                w = gate_ref[pl.ds(jnp.maximum(t - 1, 0), 1), :] * _gelu(hd[0:1])
                diag = jnp.where(eye, jnp.broadcast_to(w, (LANES, LANES)), 0.0).astype(BF16)
                wcol_new = _dot(diag, ones_sq)
        if gather_u:
            hd_scr[t % 2, (n_groups - 1) * GROUP:, :] = part
        if mix_w:
            wc_scr[...] = wcol_new
        acc[pl.ds(jnp.maximum(t - 2, 0), 1), :] = jnp.sum(acc8, axis=0, keepdims=True)

    for t0 in range(P_AHEAD):
        for j in range(N_PICK):
            expert_copy(t0, j).start(priority=j % 2)

    def main(t, _):
        step(t, True, True, True)
        return 0

    def tail(t, _):
        step(t, False, True, True)
        return 0

    lax.fori_loop(0, tt - P_AHEAD, main, 0)
    lax.fori_loop(tt - P_AHEAD, tt, tail, 0)
    step(tt, False, False, True)
    step(tt + 1, False, False, False)
    y_ref[...] = _rms(h_ref[...] + acc[...], fg_ref[...])


def _peer_ffn(eidx, gate, xn2, h2, fg, uv, tt):
    n = xn2.shape[0]
    row = lambda w: pl.BlockSpec((tt, w), lambda i: (i, 0))
    return pl.pallas_call(
        functools.partial(_peer_ffn_kernel, tt=tt),
        grid=(n // tt,),
        in_specs=[pl.BlockSpec((tt * N_PICK,), lambda i: (i,), memory_space=pltpu.SMEM),
                  row(N_PICK), row(D_MODEL), row(D_MODEL), pl.BlockSpec((1, D_MODEL), lambda i: (0, 0)),
                  pl.BlockSpec(memory_space=pl.ANY)],
        out_specs=row(D_MODEL),
        out_shape=jax.ShapeDtypeStruct((n, D_MODEL), F32),
        scratch_shapes=[pltpu.VMEM((N_SLOT, D_MODEL // LANES, N_PICK, LANES), I32), pltpu.VMEM((2, N_PICK, LANES), F32),
                        pltpu.VMEM((N_PICK, LANES), F32), pltpu.VMEM((tt, D_MODEL), F32),
                        pltpu.SemaphoreType.DMA((N_SLOT,))],
        compiler_params=pltpu.CompilerParams(dimension_semantics=("arbitrary",), vmem_limit_bytes=VMEM_LIMIT),
        name="peer_ffn",
    )(eidx.reshape(-1), gate, xn2, h2, fg, uv)


def _rope_tables(pos):
    inv_freq = ROPE_THETA ** (-jnp.arange(0, ROT_DIM, 2, dtype=F32) / ROT_DIM)
    ang = pos.astype(F32)[:, None] * inv_freq[None, :]
    cos, sin = jnp.cos(ang), jnp.sin(ang)
    n = pos.shape[0]
    rest = HEAD_DIM - ROT_DIM
    hr = ROT_DIM // 2
    c = jnp.concatenate([cos, cos, jnp.ones((n, rest), F32)], axis=1)
    a = jnp.concatenate([-sin, jnp.zeros((n, hr + rest), F32)], axis=1)
    b = jnp.concatenate([jnp.zeros((n, hr), F32), sin, jnp.zeros((n, rest), F32)], axis=1)
    return tuple(jnp.tile(z, (1, LANES // HEAD_DIM)) for z in (c, a, b))


Q_PERM = np.concatenate([np.r_[j * HEAD_DIM:(j + 1) * HEAD_DIM, (4 + j) * HEAD_DIM:(5 + j) * HEAD_DIM]
                         for j in range(4)])


def _pack_uv(u, v):
    bits = lambda a: lax.bitcast_convert_type(a.astype(jnp.bfloat16), jnp.uint16).astype(jnp.uint32)
    return lax.bitcast_convert_type((bits(u) << 16) | bits(v), I32).reshape(-1, D_MODEL // LANES, LANES)


def _prep_weights(norm_mix_g, w_in, b_fox_f, w_out, norm_mem_g, w_mem_q, w_mem_o, norm_ffn_g, peer_w_q,
                  peer_keys1, peer_keys2, peer_u, peer_v):
    sp = np.cumsum([0, 512, 128, 128, 256, 64, 4, 512, 512, 512, 8])
    a_q, a_k, a_v, a_qi, a_ki, a_w, f_q, f_k, f_v, f_f = [w_in[:, sp[i]:sp[i + 1]] for i in range(10)]
    zpad = jnp.zeros((D_MODEL, LANES - IDX_HEADS - FOX_HEADS), F32)
    w_cat = jnp.concatenate([a_q[:, Q_PERM], a_k, a_v, a_qi, a_ki, a_ki, a_w, f_f, zpad, f_q, f_k, f_v],
                            axis=1).astype(BF16)
    b_misc = jnp.concatenate([jnp.zeros((IDX_HEADS,), F32), b_fox_f.astype(F32),
                              jnp.zeros((LANES - IDX_HEADS - FOX_HEADS,), F32)])[None, :]
    r = lambda g: g.astype(F32)[None, :]
    return dict(
        g_mix=r(norm_mix_g), w_cat=w_cat, b_misc=b_misc,
        woa=w_out[:512][Q_PERM].astype(BF16), wof=w_out[512:].astype(BF16),
        g_mem=r(norm_mem_g), wq=w_mem_q.astype(BF16), wo=w_mem_o.astype(BF16), g_ffn=r(norm_ffn_g),
        wq_t=peer_w_q.T.astype(BF16), keys1=peer_keys1.astype(BF16), keys2=peer_keys2.astype(BF16),
        uv=_pack_uv(peer_u, peer_v),
    )


def _round_up(x, m):
    return (x + m - 1) // m * m


def _layer(x, q_off, past, mk_b, mv_b, w, fg, tri, cfg):
    batch, t, _ = x.shape
    n = batch * t
    x2 = x.reshape(n, D_MODEL)
    tabs = _rope_tables(q_off + jnp.arange(t, dtype=I32))
    tb_in = cfg["tb_in"]
    if tb_in > t:
        tabs = tuple(jnp.tile(z, (tb_in // t, 1)) for z in tabs)
    (aq, ak, av, qi, ki, misc, fq, fk, fv, akb, avb, kib, fkb, fvb) = _inproj(
        x2, w["g_mix"], w["w_cat"], w["b_misc"], *tabs, tb_in)
    logf = misc[:, IDX_HEADS:IDX_HEADS + FOX_HEADS].reshape(batch, t, FOX_HEADS)
    new_rows = (ak.reshape(batch, t, DSA_KV_HEADS, HEAD_DIM), av.reshape(batch, t, DSA_KV_HEADS, HEAD_DIM),
                ki[:, :IDX_DIM].reshape(batch, t, IDX_DIM), fk.reshape(batch, t, FOX_HEADS, HEAD_DIM),
                fv.reshape(batch, t, FOX_HEADS, HEAD_DIM), logf)

    kb = cfg["kb"]
    n_past = 0 if past is None else past[0].shape[1]
    n_keys = n_past + t
    lp = _round_up(n_keys, kb)

    def keys(new_b, old, width):
        new_b = new_b.reshape(batch, t, width)
        parts = [new_b] if old is None else [old.reshape(batch, n_past, -1).astype(BF16), new_b]
        if lp > n_keys:
            parts.append(jnp.zeros((batch, lp - n_keys, width), BF16))
        return parts[0] if len(parts) == 1 else jnp.concatenate(parts, axis=1)

    if past is None:
        k_all, v_all, ki_all = keys(akb, None, LANES), keys(avb, None, LANES), keys(kib, None, LANES)
        fk_all, fv_all = keys(fkb, None, 512), keys(fvb, None, 512)
        logf_all = logf
    else:
        p_k, p_v, p_ki, p_fk, p_fv, p_logf = past
        ki_dup = jnp.concatenate([p_ki, p_ki], axis=-1)
        k_all, v_all, ki_all = keys(akb, p_k, LANES), keys(avb, p_v, LANES), keys(kib, ki_dup, LANES)
        fk_all, fv_all = keys(fkb, p_fk, 512), keys(fvb, p_fv, 512)
        logf_all = jnp.concatenate([p_logf.astype(F32), logf], axis=1)

    lf = jnp.pad(logf_all, ((0, 0), (0, lp - n_keys), (0, 0)))
    lf_t = lf.reshape(batch, lp // LANES, LANES, FOX_HEADS).transpose(0, 1, 3, 2)
    d_t = _cumsum(lf_t, tri).transpose(0, 2, 1, 3).reshape(batch, FOX_HEADS, lp)
    npair = FOX_HEADS // 2
    dq_p = d_t[:, :, n_keys - t:n_keys].reshape(batch, npair, 2, t).transpose(0, 1, 3, 2)
    dk_p = d_t.reshape(batch, npair, 2, lp // kb, kb).transpose(0, 1, 3, 2, 4)

    topk = min(TOPK_MAX, n_keys // 4)
    oa = _dsa(aq, qi, misc, k_all, v_all, ki_all, tri, batch=batch, t=t, tq=cfg["tq_dsa"], kb=kb,
              n_keys=n_keys, q_off=q_off, topk=topk)
    of = _fox(fq, fk_all, fv_all, dq_p, dk_p, batch=batch, t=t, tq=cfg["tq_fox"], kb=kb,
              n_keys=n_keys, q_off=q_off)
    h2, xn2 = _post(x2, oa, of, w["woa"], w["wof"], w["g_mem"], w["wq"], mk_b, mv_b, w["wo"], w["g_ffn"],
                    batch=batch, t=t, tb=cfg["tb_post"])
    eidx, gate = _peer_sel(xn2, w["wq_t"], w["keys1"], w["keys2"], cfg["tb_sel"])
    y = _peer_ffn(eidx, gate, xn2, h2, fg, w["uv"], cfg["tt_ffn"])
    return y.reshape(batch, t, D_MODEL), new_rows


def _config(t, n_tokens, has_past):
    if has_past:
        return dict(tb_in=n_tokens, kb=384, tq_dsa=t, tq_fox=t, tb_post=t, tb_sel=n_tokens, tt_ffn=min(128, n_tokens))
    return dict(tb_in=min(512, t), kb=min(512, t), tq_dsa=min(128, t), tq_fox=min(256, t), tb_post=min(256, t),
                tb_sel=min(256, n_tokens), tt_ffn=min(128, n_tokens))


def kernel(x_prompt, x_sample, cache_dsa_k, cache_dsa_v, cache_dsa_kidx, cache_fox_k, cache_fox_v, cache_fox_logf, cache_mem_k, cache_mem_v, mem_prompt, norm_mix_g, w_in, b_fox_f, w_out, mem_norm_g, w_mem_kv, norm_mem_g, w_mem_q, w_mem_o, norm_ffn_g, peer_w_q, peer_keys1, peer_keys2, peer_u, peer_v, final_norm_g):
    depth = w_in.shape[0]
    assert depth == 1, "single-layer trunk"
    l = 0
    bp, tp, _ = x_prompt.shape
    bs, ts, _ = x_sample.shape
    past_len = cache_dsa_k.shape[2]
    w = _prep_weights(norm_mix_g[l], w_in[l], b_fox_f[l], w_out[l], norm_mem_g[l], w_mem_q[l], w_mem_o[l],
                      norm_ffn_g[l], peer_w_q[l], peer_keys1[l], peer_keys2[l], peer_u[l], peer_v[l])
    fg = final_norm_g.astype(F32)[None, :]
    ii = np.arange(LANES)
    tri = jnp.asarray(ii[:, None] <= ii[None, :], BF16)

    mk, mv, mkb, mvb = _memkv(mem_prompt.reshape(bp * N_MEM, D_MODEL), mem_norm_g[l].astype(F32)[None, :],
                              w_mem_kv[l].astype(BF16))
    y_p, new_p = _layer(x_prompt, 0, None, mkb.reshape(bp, N_MEM, D_MODEL), mvb.reshape(bp, N_MEM, D_MODEL),
                        w, fg, tri, _config(tp, bp * tp, False))
    past = (cache_dsa_k[l], cache_dsa_v[l], cache_dsa_kidx[l], cache_fox_k[l], cache_fox_v[l], cache_fox_logf[l])
    y_s, new_s = _layer(x_sample, past_len, past, cache_mem_k[l].reshape(bs, N_MEM, D_MODEL).astype(BF16),
                        cache_mem_v[l].reshape(bs, N_MEM, D_MODEL).astype(BF16), w, fg, tri,
                        _config(ts, bs * ts, True))
    st = lambda a: a[None]
    mem_shape = (bp, N_MEM, MEM_HEADS, MEM_HEAD_DIM)
    return (y_p, y_s) + tuple(st(a) for a in new_p) + (st(mk.reshape(mem_shape)), st(mv.reshape(mem_shape))) \
        + tuple(st(a) for a in new_s)
```

```python
import functools

import jax
import jax.numpy as jnp
import numpy as np
from jax import lax
from jax.experimental import pallas as pl
from jax.experimental.pallas import tpu as pltpu

D_MODEL = 1024
CHUNK_SHIFT = 6
DSA_HEADS = 8
FOX_HEADS = 8
HEAD_DIM = 64
DSA_KV_HEADS = 2
IDX_HEADS = 4
IDX_DIM = 64
TOPK_MAX = 256
ROPE_THETA = 500000.0
ROT_DIM = HEAD_DIM // 4
N_MEM = 256
MEM_HEADS = 4
MEM_HEAD_DIM = D_MODEL // MEM_HEADS
PEER_HEADS = 8
PEER_KEY_DIM = 256
N_KEYS = 128
PEER_TOPK = 16
EPS = 1e-6
LANES = 128
VMEM_LIMIT = 48 * 1024 * 1024

F32 = jnp.float32
BF16 = jnp.bfloat16
I32 = jnp.int32

INT_MIN = int(np.iinfo(np.int32).min)
_NEG_INF_BITS = int(np.array(-np.inf, np.float32).view(np.int32))
KEY_NEG_INF = _NEG_INF_BITS ^ 0x7FFFFFFF
KEY_NEG_INF = KEY_NEG_INF - (1 << 32) if KEY_NEG_INF >= (1 << 31) else KEY_NEG_INF
MASK_VALUE = -1e30
MASK_BITS = int(np.array(MASK_VALUE, np.float32).view(np.int32))

SEG_Q = (0, 512)
SEG_K = (512, 640)
SEG_V = (640, 768)
SEG_QI = (768, 1024)
SEG_KI = (1024, 1152)
SEG_MISC = (1152, 1280)
SEG_FQ = (1280, 1792)
SEG_FK = (1792, 2304)
SEG_FV = (2304, 2816)
IN_COLS = 2816

CAND = [(a, b) for a in range(PEER_TOPK) for b in range(PEER_TOPK) if (a + 1) * (b + 1) <= PEER_TOPK]
N_CAND = len(CAND)


def _dot(a, b):
    return jnp.dot(a, b, preferred_element_type=F32)


def _dot_t(a, b):
    return lax.dot_general(a, b, (((1,), (1,)), ((), ())), preferred_element_type=F32)


def _rms(x, g):
    return x * lax.rsqrt(jnp.mean(x * x, axis=-1, keepdims=True) + EPS) * g


def _inproj_kernel(x_ref, g_ref, w_ref, b_ref, cos_ref, sa_ref, sb_ref,
                   aq_ref, ak_ref, av_ref, qi_ref, ki_ref, misc_ref, fq_ref, fk_ref, fv_ref,
                   akb_ref, avb_ref, kib_ref, fkb_ref, fvb_ref):
    xn = _rms(x_ref[...], g_ref[...]).astype(BF16)
    cos, sa, sb = cos_ref[...], sa_ref[...], sb_ref[...]

    def proj(seg):
        return _dot(xn, w_ref[:, seg[0]:seg[1]])

    def rope(y):
        outs = []
        for j in range(y.shape[1] // LANES):
            s = y[:, j * LANES:(j + 1) * LANES]
            outs.append(s * cos + pltpu.roll(s, LANES - ROT_DIM // 2, 1) * sa
                        + pltpu.roll(s, ROT_DIM // 2, 1) * sb)
        return outs[0] if len(outs) == 1 else jnp.concatenate(outs, axis=1)

    scale = HEAD_DIM ** -0.5
    aq_ref[...] = (rope(proj(SEG_Q)) * scale).astype(BF16)
    ak = rope(proj(SEG_K))
    ak_ref[...] = ak
    akb_ref[...] = ak.astype(BF16)
    av = proj(SEG_V)
    av_ref[...] = av
    avb_ref[...] = av.astype(BF16)
    qi_ref[...] = rope(proj(SEG_QI)).astype(BF16)
    ki = rope(proj(SEG_KI))
    ki_ref[...] = ki
    kib_ref[...] = ki.astype(BF16)
    z = proj(SEG_MISC)
    zb = z + b_ref[...]
    logf = jnp.minimum(zb, 0.0) - jnp.log1p(jnp.exp(-jnp.abs(zb)))
    lane = lax.broadcasted_iota(I32, z.shape, 1)
    misc_ref[...] = jnp.where((lane >= IDX_HEADS) & (lane < IDX_HEADS + FOX_HEADS), logf, z)
    fq_ref[...] = (proj(SEG_FQ) * scale).astype(BF16)
    fk = proj(SEG_FK)
    fk_ref[...] = fk
    fkb_ref[...] = fk.astype(BF16)
    fv = proj(SEG_FV)
    fv_ref[...] = fv
    fvb_ref[...] = fv.astype(BF16)


def _inproj(x2, g, w_cat, b_misc, cos, sa, sb, tb):
    n = x2.shape[0]
    ntab = cos.shape[0] // tb
    row = lambda w: pl.BlockSpec((tb, w), lambda i: (i, 0))
    tab = pl.BlockSpec((tb, LANES), lambda i: (i % ntab, 0))
    const = lambda s: pl.BlockSpec(s, lambda i: (0, 0))
    widths = [(512, BF16), (128, F32), (128, F32), (256, BF16), (128, F32), (128, F32),
              (512, BF16), (512, F32), (512, F32),
              (128, BF16), (128, BF16), (128, BF16), (512, BF16), (512, BF16)]
    return pl.pallas_call(
        _inproj_kernel,
        grid=(n // tb,),
        in_specs=[row(D_MODEL), const((1, D_MODEL)), const((D_MODEL, IN_COLS)), const((1, LANES)), tab, tab, tab],
        out_specs=[row(w) for w, _ in widths],
        out_shape=[jax.ShapeDtypeStruct((n, w), dt) for w, dt in widths],
        compiler_params=pltpu.CompilerParams(dimension_semantics=("arbitrary",), vmem_limit_bytes=VMEM_LIMIT),
        name="inproj",
    )(x2, g, w_cat, b_misc, cos, sa, sb)


def _split3(x):
    hi = x.astype(BF16)
    r1 = x - hi.astype(F32)
    mid = r1.astype(BF16)
    lo = (r1 - mid.astype(F32)).astype(BF16)
    return hi, mid, lo


def _cumsum_kernel(x_ref, tri_ref, o_ref):
    tri = tri_ref[...]

    def body(c, carry):
        hi, mid, lo = _split3(x_ref[0, c])
        out = _dot(hi, tri) + _dot(mid, tri) + _dot(lo, tri) + carry
        o_ref[0, c] = out
        return out[:, LANES - 1:LANES]

    lax.fori_loop(0, x_ref.shape[1], body, jnp.zeros((FOX_HEADS, 1), F32))


def _cumsum(logf_t, tri):
    b, nchunk = logf_t.shape[:2]
    spec = pl.BlockSpec((1, nchunk, FOX_HEADS, LANES), lambda i: (i, 0, 0, 0))
    return pl.pallas_call(
        _cumsum_kernel,
        grid=(b,),
        in_specs=[spec, pl.BlockSpec((LANES, LANES), lambda i: (0, 0))],
        out_specs=spec,
        out_shape=jax.ShapeDtypeStruct(logf_t.shape, F32),
        name="cumsum",
    )(logf_t, tri)


def _dsa_kernel(aq_ref, qi_ref, misc_ref, k_ref, v_ref, ki_ref, tri_ref, o_ref, s_ref,
                *, tq, kb, n_keys, q_off, topk):
    i = pl.program_id(1)
    sub = kb // LANES
    q_last = q_off + (i + 1) * tq - 1
    adm_len = jnp.minimum(n_keys, ((q_last >> CHUNK_SHIFT) + 1) << CHUNK_SHIFT)
    nkb = (adm_len + kb - 1) // kb
    qchunk = (q_off + i * tq + lax.broadcasted_iota(I32, (tq, 1), 0)) >> CHUNK_SHIFT
    lane = lax.broadcasted_iota(I32, (tq, LANES), 1)
    lo_half = lane < HEAD_DIM

    zero = jnp.zeros((), BF16)
    qh = []
    for j in range(IDX_HEADS // 2):
        slab = qi_ref[:, j * LANES:(j + 1) * LANES]
        qh += [jnp.where(lo_half, slab, zero), jnp.where(lo_half, zero, slab)]
    wi = [misc_ref[:, h:h + 1] for h in range(IDX_HEADS)]

    def p1(c, _):
        off = pl.multiple_of(c * kb, kb)
        kib = ki_ref[0, pl.ds(off, kb), :]
        sc = jnp.zeros((tq, kb), F32)
        for h in range(IDX_HEADS):
            sc = sc + jnp.maximum(_dot_t(qh[h], kib), 0.0) * wi[h]
        kpos = off + lax.broadcasted_iota(I32, (1, kb), 1)
        ok = ((kpos >> CHUNK_SHIFT) <= qchunk) & (kpos < n_keys)
        bits = lax.bitcast_convert_type(sc, I32)
        key = jnp.where(ok, bits ^ ((bits >> 31) & 0x7FFFFFFF), KEY_NEG_INF)
        for j in range(sub):
            s_ref[c * sub + j] = key[:, j * LANES:(j + 1) * LANES]
        return 0

    lax.fori_loop(0, nkb, p1, 0)

    def count_ge(trial):
        tb = jnp.broadcast_to(trial, (tq, LANES))

        def body(c, acc):
            for j in range(sub):
                acc = acc + (s_ref[c * sub + j] >= tb).astype(I32)
            return acc

        acc = lax.fori_loop(0, nkb, body, jnp.zeros((tq, LANES), I32))
        return jnp.sum(acc, axis=1, keepdims=True)

    def bisect(it, cand):
        trial = cand | (jnp.int32(1) << (31 - it))
        cnt = count_ge(trial ^ INT_MIN)
        return jnp.where(cnt >= topk, trial, cand)

    vstar = lax.fori_loop(0, 32, bisect, jnp.zeros((tq, 1), I32)) ^ INT_MIN
    need = (topk - count_ge(vstar + 1)).astype(F32)

    vb = jnp.broadcast_to(vstar, (tq, LANES))
    tri = tri_ref[...]

    def p2b(c, carry):
        for j in range(sub):
            blk = s_ref[c * sub + j]
            eq = blk == vb
            pref = _dot(jnp.where(eq, 1.0, 0.0).astype(BF16), tri)
            sel = ((blk > vb) | (eq & (pref + carry <= need))) & (blk > KEY_NEG_INF)
            s_ref[c * sub + j] = jnp.where(sel, 0, MASK_BITS)
            carry = carry + pref[:, LANES - 1:LANES]
        return carry

    lax.fori_loop(0, nkb, p2b, jnp.zeros((tq, 1), F32))

    rep = DSA_HEADS // DSA_KV_HEADS
    res = []
    for g in range(DSA_KV_HEADS):
        in_g = lo_half if g == 0 else jnp.logical_not(lo_half)
        q_all = jnp.concatenate(
            [jnp.where(in_g, aq_ref[:, r * LANES:(r + 1) * LANES], zero) for r in range(rep)], axis=0)

        def p3(c, carry):
            m, l, acc = carry
            off = pl.multiple_of(c * kb, kb)
            lg = _dot_t(q_all, k_ref[0, pl.ds(off, kb), :])
            bias = jnp.concatenate(
                [lax.bitcast_convert_type(s_ref[c * sub + j], F32) for j in range(sub)], axis=1)
            lg = lg + jnp.concatenate([bias] * rep, axis=0)
            mn = jnp.maximum(m, jnp.max(lg, axis=1, keepdims=True))
            a = jnp.exp(m - mn)
            p = jnp.exp(lg - mn)
            l = a * l + jnp.sum(p, axis=1, keepdims=True)
            acc = a * acc + _dot(p.astype(BF16), v_ref[0, pl.ds(off, kb), :])
            return mn, l, acc

        init = (jnp.full((rep * tq, 1), MASK_VALUE, F32), jnp.zeros((rep * tq, 1), F32),
                jnp.zeros((rep * tq, LANES), F32))
        _, l, acc = lax.fori_loop(0, nkb, p3, init)
        res.append(acc / l)
    for r in range(rep):
        o_ref[:, r * LANES:(r + 1) * LANES] = jnp.where(
            lo_half, res[0][r * tq:(r + 1) * tq], res[1][r * tq:(r + 1) * tq]).astype(BF16)


def _dsa(aq, qi, misc, k_b, v_b, ki_b, tri, *, batch, t, tq, kb, n_keys, q_off, topk):
    lp = k_b.shape[1]
    nq = t // tq
    qrow = lambda w: pl.BlockSpec((tq, w), lambda b, i: (b * nq + i, 0))
    keys = pl.BlockSpec((1, lp, LANES), lambda b, i: (b, 0, 0))
    kern = functools.partial(_dsa_kernel, tq=tq, kb=kb, n_keys=n_keys, q_off=q_off, topk=topk)
    return pl.pallas_call(
        kern,
        grid=(batch, nq),
        in_specs=[qrow(512), qrow(256), qrow(LANES), keys, keys, keys,
                  pl.BlockSpec((LANES, LANES), lambda b, i: (0, 0))],
        out_specs=qrow(512),
        out_shape=jax.ShapeDtypeStruct((batch * t, 512), BF16),
        scratch_shapes=[pltpu.VMEM((lp // LANES, tq, LANES), I32)],
        compiler_params=pltpu.CompilerParams(dimension_semantics=("arbitrary", "arbitrary"),
                                             vmem_limit_bytes=VMEM_LIMIT),
        name="dsa",
    )(aq, qi, misc, k_b, v_b, ki_b, tri)


def _fox_kernel(q_ref, k_ref, v_ref, dq_ref, dk_ref, o_ref, *, tq, kb, n_keys, q_off):
    i = pl.program_id(2)
    nkb = (jnp.minimum(n_keys, q_off + (i + 1) * tq) + kb - 1) // kb
    lane = lax.broadcasted_iota(I32, (tq, LANES), 1)
    lo_half = lane < HEAD_DIM
    zero = jnp.zeros((), BF16)
    q = q_ref[...]
    q_all = jnp.concatenate([jnp.where(lo_half, q, zero), jnp.where(lo_half, zero, q)], axis=0)
    dq = dq_ref[0, 0]
    qpos = q_off + i * tq + lax.broadcasted_iota(I32, (tq, 1), 0)

    def body(c, carry, masked):
        off = pl.multiple_of(c * kb, kb)
        lg = _dot_t(q_all, k_ref[0, pl.ds(off, kb), :])
        vblk = v_ref[0, pl.ds(off, kb), :]
        dk = dk_ref[0, 0, c]
        if masked:
            kpos = off + lax.broadcasted_iota(I32, (1, kb), 1)
            ok = (kpos <= qpos) & (kpos < n_keys)
        out = []
        for e in range(2):
            m, l, acc = carry[e]
            lge = lg[e * tq:(e + 1) * tq] + dq[:, e:e + 1] - dk[e:e + 1, :]
            if masked:
                lge = jnp.where(ok, lge, MASK_VALUE)
            mn = jnp.maximum(m, jnp.max(lge, axis=1, keepdims=True))
            a = jnp.exp(m - mn)
            p = jnp.exp(lge - mn)
            l = a * l + jnp.sum(p, axis=1, keepdims=True)
            acc = a * acc + _dot(p.astype(BF16), vblk)
            out.append((mn, l, acc))
        return tuple(out)

    one = (jnp.full((tq, 1), MASK_VALUE, F32), jnp.zeros((tq, 1), F32), jnp.zeros((tq, LANES), F32))
    n_full = jnp.minimum((q_off + i * tq + 1) // kb, n_keys // kb)
    carry = lax.fori_loop(0, n_full, functools.partial(body, masked=False), (one, one))
    (_, l0, a0), (_, l1, a1) = lax.fori_loop(n_full, nkb, functools.partial(body, masked=True), carry)
    o_ref[...] = jnp.where(lo_half, a0 / l0, a1 / l1).astype(BF16)


def _fox(fq, k_b, v_b, dq_p, dk_p, *, batch, t, tq, kb, n_keys, q_off):
    lp = k_b.shape[1]
    nq = t // tq
    npair = FOX_HEADS // 2
    kern = functools.partial(_fox_kernel, tq=tq, kb=kb, n_keys=n_keys, q_off=q_off)
    kv = pl.BlockSpec((1, lp, LANES), lambda b, p, i: (b, 0, p))
    qo = pl.BlockSpec((tq, LANES), lambda b, p, i: (b * nq + i, p))
    return pl.pallas_call(
        kern,
        grid=(batch, npair, nq),
        in_specs=[qo, kv, kv,
                  pl.BlockSpec((1, 1, tq, 2), lambda b, p, i: (b, p, i, 0)),
                  pl.BlockSpec((1, 1, lp // kb, 2, kb), lambda b, p, i: (b, p, 0, 0, 0))],
        out_specs=qo,
        out_shape=jax.ShapeDtypeStruct((batch * t, FOX_HEADS * HEAD_DIM), BF16),
        compiler_params=pltpu.CompilerParams(dimension_semantics=("arbitrary",) * 3,
                                             vmem_limit_bytes=VMEM_LIMIT),
        name="fox",
    )(fq, k_b, v_b, dq_p, dk_p)


def _memkv_kernel(x_ref, g_ref, w_ref, k_ref, v_ref, kb_ref, vb_ref):
    xn = _rms(x_ref[...], g_ref[...]).astype(BF16)
    k = _dot(xn, w_ref[:, :D_MODEL])
    v = _dot(xn, w_ref[:, D_MODEL:])
    k_ref[...] = k
    v_ref[...] = v
    kb_ref[...] = k.astype(BF16)
    vb_ref[...] = v.astype(BF16)


def _memkv(mem2, g, w_kv):
    n = mem2.shape[0]
    row = pl.BlockSpec((N_MEM, D_MODEL), lambda i: (i, 0))
    return pl.pallas_call(
        _memkv_kernel,
        grid=(n // N_MEM,),
        in_specs=[row, pl.BlockSpec((1, D_MODEL), lambda i: (0, 0)),
                  pl.BlockSpec((D_MODEL, 2 * D_MODEL), lambda i: (0, 0))],
        out_specs=[row] * 4,
        out_shape=[jax.ShapeDtypeStruct((n, D_MODEL), dt) for dt in (F32, F32, BF16, BF16)],
        compiler_params=pltpu.CompilerParams(dimension_semantics=("arbitrary",), vmem_limit_bytes=VMEM_LIMIT),
        name="memkv",
    )(mem2, g, w_kv)


def _post_kernel(x_ref, oa_ref, of_ref, woa_ref, wof_ref, gm_ref, wq_ref, mk_ref, mv_ref, wo_ref, gf_ref,
                 h_ref, xn_ref):
    h = x_ref[...] + _dot(oa_ref[...], woa_ref[...]) + _dot(of_ref[...], wof_ref[...])
    hn = _rms(h, gm_ref[...]).astype(BF16)
    q = (_dot(hn, wq_ref[...]) * (MEM_HEAD_DIM ** -0.5)).astype(BF16)
    outs = []
    for hd in range(MEM_HEADS):
        cs = slice(hd * MEM_HEAD_DIM, (hd + 1) * MEM_HEAD_DIM)
        lg = _dot_t(q[:, cs], mk_ref[0, :, cs])
        p = jnp.exp(lg - jnp.max(lg, axis=1, keepdims=True))
        p = p / jnp.sum(p, axis=1, keepdims=True)
        outs.append(_dot(p.astype(BF16), mv_ref[0, :, cs]))
    o = jnp.concatenate(outs, axis=1).astype(BF16)
    h = h + _dot(o, wo_ref[...])
    h_ref[...] = h
    xn_ref[...] = _rms(h, gf_ref[...])


def _post(x2, oa, of, woa, wof, gm, wq, mk_b, mv_b, wo, gf, *, batch, t, tb):
    nt = t // tb
    row = lambda w: pl.BlockSpec((tb, w), lambda b, i: (b * nt + i, 0))
    const = lambda s: pl.BlockSpec(s, lambda b, i: (0, 0))
    mem = pl.BlockSpec((1, N_MEM, D_MODEL), lambda b, i: (b, 0, 0))
    return pl.pallas_call(
        _post_kernel,
        grid=(batch, nt),
        in_specs=[row(D_MODEL), row(512), row(512), const((512, D_MODEL)), const((512, D_MODEL)),
                  const((1, D_MODEL)), const((D_MODEL, D_MODEL)), mem, mem, const((D_MODEL, D_MODEL)),
                  const((1, D_MODEL))],
        out_specs=[row(D_MODEL), row(D_MODEL)],
        out_shape=[jax.ShapeDtypeStruct((batch * t, D_MODEL), F32)] * 2,
        compiler_params=pltpu.CompilerParams(dimension_semantics=("arbitrary", "arbitrary"),
                                             vmem_limit_bytes=VMEM_LIMIT),
        name="post",
    )(x2, oa, of, woa, wof, gm, wq, mk_b, mv_b, wo, gf)


def _top16(s):
    r = s.shape[0]
    row = lax.broadcasted_iota(I32, s.shape, 0).astype(F32)
    vals, ids = [], []
    for _ in range(PEER_TOPK):
        m = jnp.max(s, axis=0, keepdims=True)
        am = jnp.min(jnp.where(s == m, row, float(r)), axis=0, keepdims=True)
        vals.append(m)
        ids.append(am)
        s = jnp.where(row == am, -jnp.inf, s)
    return jnp.concatenate(vals, axis=0), jnp.concatenate(ids, axis=0)


def _peer_sel_kernel(x_ref, wqt_ref, k1_ref, k2_ref, eidx_ref, gate_ref, qt_scr, e_scr, g_scr):
    half = PEER_KEY_DIM // 2
    qt_scr[...] = _dot_t(wqt_ref[...], x_ref[...].astype(BF16)).astype(BF16)
    n = qt_scr.shape[1]
    pad = (-N_CAND) % 8

    def head(h, _):
        r0 = pl.multiple_of(h * PEER_KEY_DIM, PEER_KEY_DIM)
        v1, i1 = _top16(_dot(k1_ref[h], qt_scr[pl.ds(r0, half), :]))
        v2, i2 = _top16(_dot(k2_ref[h], qt_scr[pl.ds(r0 + half, half), :]))
        cand = jnp.concatenate([v1[a:a + 1] + v2[b:b + 1] for a, b in CAND]
                               + [jnp.full((pad, n), -jnp.inf, F32)], axis=0)
        cidx = jnp.concatenate([i1[a:a + 1] * float(N_KEYS) + i2[b:b + 1] for a, b in CAND]
                               + [jnp.zeros((pad, n), F32)], axis=0)
        sv, sr = _top16(cand)
        row = lax.broadcasted_iota(I32, cand.shape, 0).astype(F32)
        eid = jnp.concatenate(
            [jnp.sum(jnp.where(row == sr[k:k + 1], cidx, 0.0), axis=0, keepdims=True) for k in range(PEER_TOPK)],
            axis=0)
        p = jnp.exp(sv - sv[0:1])
        o0 = pl.multiple_of(h * PEER_TOPK, PEER_TOPK)
        e_scr[pl.ds(o0, PEER_TOPK), :] = eid
        g_scr[pl.ds(o0, PEER_TOPK), :] = p / jnp.sum(p, axis=0, keepdims=True)
        return 0

    lax.fori_loop(0, PEER_HEADS, head, 0)
    eidx_ref[...] = e_scr[...].T.astype(I32)
    gate_ref[...] = g_scr[...].T


def _peer_sel(xn2, wq_t, keys1, keys2, tb):
    n = xn2.shape[0]
    kspec = pl.BlockSpec((PEER_HEADS, N_KEYS, PEER_KEY_DIM // 2), lambda i: (0, 0, 0))
    out = pl.BlockSpec((tb, PEER_HEADS * PEER_TOPK), lambda i: (i, 0))
    return pl.pallas_call(
        _peer_sel_kernel,
        grid=(n // tb,),
        in_specs=[pl.BlockSpec((tb, D_MODEL), lambda i: (i, 0)),
                  pl.BlockSpec((PEER_HEADS * PEER_KEY_DIM, D_MODEL), lambda i: (0, 0)), kspec, kspec],
        out_specs=[out, out],
        out_shape=[jax.ShapeDtypeStruct((n, PEER_HEADS * PEER_TOPK), I32),
                   jax.ShapeDtypeStruct((n, PEER_HEADS * PEER_TOPK), F32)],
        scratch_shapes=[pltpu.VMEM((PEER_HEADS * PEER_KEY_DIM, tb), BF16)]
        + [pltpu.VMEM((PEER_HEADS * PEER_TOPK, tb), F32)] * 2,
        compiler_params=pltpu.CompilerParams(dimension_semantics=("arbitrary",), vmem_limit_bytes=VMEM_LIMIT),
        name="peer_sel",
    )(xn2, wq_t, keys1, keys2)


N_PICK = PEER_HEADS * PEER_TOPK
P_AHEAD = 6
N_SLOT = P_AHEAD + 4
GROUP = 8
HI_MASK = -65536


def _gelu(x):
    return 0.5 * x * (1.0 + lax.erf(x * (2.0 ** -0.5)))


def _peer_ffn_kernel(eidx_ref, gate_ref, x_ref, h_ref, fg_ref, ones_ref, eye_ref, uv_ref, y_ref,
                     buf, hd_scr, w_scr, wc_scr, acc, sem, *, tt):
    @pl.when(pl.program_id(0) == 0)
    def _():
        buf[...] = jnp.zeros(buf.shape, I32)
        hd_scr[...] = jnp.zeros(hd_scr.shape, F32)
        w_scr[...] = jnp.zeros(w_scr.shape, F32)
        wc_scr[...] = jnp.zeros(wc_scr.shape, F32)

    n_tiles = D_MODEL // LANES
    n_groups = N_PICK // GROUP
    c1_at, c2_at = 1, 10

    def expert_copy(t, j, slot):
        return pltpu.make_async_copy(uv_ref.at[eidx_ref[t * N_PICK + j]], buf.at[slot, :, j, :], sem.at[slot])

    def packed_rows(slot, rows):
        return jnp.concatenate([buf[slot, s, rows, :] for s in range(n_tiles)], axis=1)

    def lane_tiles_sum(a):
        out = a[:, 0:LANES]
        for c in range(1, n_tiles):
            out = out + a[:, c * LANES:(c + 1) * LANES]
        return out

    def step(t, r, issue, gather_u, mix1, mix2):
        slot_v = (r - 3) % N_SLOT
        par = r % 2
        if gather_u:
            for _ in range(N_PICK):
                pltpu.make_async_copy(uv_ref.at[0], buf.at[r, :, 0, :], sem.at[r]).wait()
            xrow = jnp.broadcast_to(x_ref[pl.ds(t, 1), :], (GROUP, D_MODEL))
        acc8 = jnp.zeros((GROUP, D_MODEL), F32)
        part = w_new = wcol_new = None
        for k in range(n_groups):
            rows = slice(k * GROUP, (k + 1) * GROUP)
            if issue:
                for e in range(GROUP):
                    expert_copy(t + P_AHEAD, k * GROUP + e, (r + P_AHEAD) % N_SLOT).start(priority=e % 2)
            if gather_u:
                if part is not None:
                    hd_scr[par, (k - 1) * GROUP:k * GROUP, :] = part
                u = lax.bitcast_convert_type(packed_rows(r, rows) & HI_MASK, F32)
                part = lane_tiles_sum(u * xrow)
            v = lax.bitcast_convert_type(packed_rows(slot_v, rows) << 16, F32)
            acc8 = acc8 + v * jnp.concatenate([wc_scr[rows, :]] * n_tiles, axis=1)
            if mix1 and k == c1_at:
                hdp = hd_scr[1 - par]
                hi = hdp.astype(BF16)
                lo = (hdp - hi.astype(F32)).astype(BF16)
                ones8 = ones_ref[0:GROUP, :]
                hd = _dot_t(ones8, hi) + _dot_t(ones8, lo)
                w_new = gate_ref[pl.ds(jnp.maximum(t - 1, 0), 1), :] * _gelu(hd)
            if mix2 and k == c2_at:
                diag = (eye_ref[...] * jnp.broadcast_to(w_scr[0:1, :], (LANES, LANES))).astype(BF16)
                wcol_new = _dot(diag, ones_ref[...])
        if gather_u:
            hd_scr[par, (n_groups - 1) * GROUP:, :] = part
        if mix2:
            wc_scr[...] = wcol_new
        if mix1:
            w_scr[...] = w_new
        acc[pl.ds(jnp.maximum(t - 3, 0), 1), :] = jnp.sum(acc8, axis=0, keepdims=True)

    for t0 in range(P_AHEAD):
        for j in range(N_PICK):
            expert_copy(t0, j, t0).start(priority=j % 2)

    n_round = (tt - P_AHEAD) // N_SLOT

    def one_round(q, _):
        for r in range(N_SLOT):
            step(q * N_SLOT + r, r, True, True, True, True)
        return 0

    lax.fori_loop(0, n_round, one_round, 0)
    for t in range(n_round * N_SLOT, tt + 3):
        step(t, t % N_SLOT, t + P_AHEAD < tt, t < tt, t <= tt, t <= tt + 1)
    y_ref[...] = _rms(h_ref[...] + acc[...], fg_ref[...])


def _peer_ffn(eidx, gate, xn2, h2, fg, uv, tt):
    n = xn2.shape[0]
    row = lambda w: pl.BlockSpec((tt, w), lambda i: (i, 0))
    const = lambda s: pl.BlockSpec(s, lambda i: (0, 0))
    return pl.pallas_call(
        functools.partial(_peer_ffn_kernel, tt=tt),
        grid=(n // tt,),
        in_specs=[pl.BlockSpec((tt * N_PICK,), lambda i: (i,), memory_space=pltpu.SMEM),
                  row(N_PICK), row(D_MODEL), row(D_MODEL), const((1, D_MODEL)), const((LANES, LANES)),
                  const((LANES, LANES)), pl.BlockSpec(memory_space=pl.ANY)],
        out_specs=row(D_MODEL),
        out_shape=jax.ShapeDtypeStruct((n, D_MODEL), F32),
        scratch_shapes=[pltpu.VMEM((N_SLOT, D_MODEL // LANES, N_PICK, LANES), I32), pltpu.VMEM((2, N_PICK, LANES), F32),
                        pltpu.VMEM((GROUP, LANES), F32), pltpu.VMEM((N_PICK, LANES), F32),
                        pltpu.VMEM((tt, D_MODEL), F32), pltpu.SemaphoreType.DMA((N_SLOT,))],
        compiler_params=pltpu.CompilerParams(dimension_semantics=("arbitrary",), vmem_limit_bytes=VMEM_LIMIT),
        name="peer_ffn",
    )(eidx.reshape(-1), gate, xn2, h2, fg, jnp.ones((LANES, LANES), BF16), jnp.eye(LANES, dtype=F32), uv)


def _rope_tables(pos):
    inv_freq = ROPE_THETA ** (-jnp.arange(0, ROT_DIM, 2, dtype=F32) / ROT_DIM)
    ang = pos.astype(F32)[:, None] * inv_freq[None, :]
    cos, sin = jnp.cos(ang), jnp.sin(ang)
    n = pos.shape[0]
    rest = HEAD_DIM - ROT_DIM
    hr = ROT_DIM // 2
    c = jnp.concatenate([cos, cos, jnp.ones((n, rest), F32)], axis=1)
    a = jnp.concatenate([-sin, jnp.zeros((n, hr + rest), F32)], axis=1)
    b = jnp.concatenate([jnp.zeros((n, hr), F32), sin, jnp.zeros((n, rest), F32)], axis=1)
    return tuple(jnp.tile(z, (1, LANES // HEAD_DIM)) for z in (c, a, b))


Q_PERM = np.concatenate([np.r_[j * HEAD_DIM:(j + 1) * HEAD_DIM, (4 + j) * HEAD_DIM:(5 + j) * HEAD_DIM]
                         for j in range(4)])


def _pack_uv(u, v):
    bits = lambda a: lax.bitcast_convert_type(a.astype(jnp.bfloat16), jnp.uint16).astype(jnp.uint32)
    return lax.bitcast_convert_type((bits(u) << 16) | bits(v), I32).reshape(-1, D_MODEL // LANES, LANES)


def _prep_weights(norm_mix_g, w_in, b_fox_f, w_out, norm_mem_g, w_mem_q, w_mem_o, norm_ffn_g, peer_w_q,
                  peer_keys1, peer_keys2, peer_u, peer_v):
    sp = np.cumsum([0, 512, 128, 128, 256, 64, 4, 512, 512, 512, 8])
    a_q, a_k, a_v, a_qi, a_ki, a_w, f_q, f_k, f_v, f_f = [w_in[:, sp[i]:sp[i + 1]] for i in range(10)]
    zpad = jnp.zeros((D_MODEL, LANES - IDX_HEADS - FOX_HEADS), F32)
    w_cat = jnp.concatenate([a_q[:, Q_PERM], a_k, a_v, a_qi, a_ki, a_ki, a_w, f_f, zpad, f_q, f_k, f_v],
                            axis=1).astype(BF16)
    b_misc = jnp.concatenate([jnp.zeros((IDX_HEADS,), F32), b_fox_f.astype(F32),
                              jnp.zeros((LANES - IDX_HEADS - FOX_HEADS,), F32)])[None, :]
    r = lambda g: g.astype(F32)[None, :]
    return dict(
        g_mix=r(norm_mix_g), w_cat=w_cat, b_misc=b_misc,
        woa=w_out[:512][Q_PERM].astype(BF16), wof=w_out[512:].astype(BF16),
        g_mem=r(norm_mem_g), wq=w_mem_q.astype(BF16), wo=w_mem_o.astype(BF16), g_ffn=r(norm_ffn_g),
        wq_t=peer_w_q.T.astype(BF16), keys1=peer_keys1.astype(BF16), keys2=peer_keys2.astype(BF16),
        uv=_pack_uv(peer_u, peer_v),
    )


def _round_up(x, m):
    return (x + m - 1) // m * m


def _layer(x, q_off, past, mk_b, mv_b, w, fg, tri, cfg):
    batch, t, _ = x.shape
    n = batch * t
    x2 = x.reshape(n, D_MODEL)
    tabs = _rope_tables(q_off + jnp.arange(t, dtype=I32))
    tb_in = cfg["tb_in"]
    if tb_in > t:
        tabs = tuple(jnp.tile(z, (tb_in // t, 1)) for z in tabs)
    (aq, ak, av, qi, ki, misc, fq, fk, fv, akb, avb, kib, fkb, fvb) = _inproj(
        x2, w["g_mix"], w["w_cat"], w["b_misc"], *tabs, tb_in)
    logf = misc[:, IDX_HEADS:IDX_HEADS + FOX_HEADS].reshape(batch, t, FOX_HEADS)
    new_rows = (ak.reshape(batch, t, DSA_KV_HEADS, HEAD_DIM), av.reshape(batch, t, DSA_KV_HEADS, HEAD_DIM),
                ki[:, :IDX_DIM].reshape(batch, t, IDX_DIM), fk.reshape(batch, t, FOX_HEADS, HEAD_DIM),
                fv.reshape(batch, t, FOX_HEADS, HEAD_DIM), logf)

    kb = cfg["kb"]
    n_past = 0 if past is None else past[0].shape[1]
    n_keys = n_past + t
    lp = _round_up(n_keys, kb)

    def keys(new_b, old, width):
        new_b = new_b.reshape(batch, t, width)
        parts = [new_b] if old is None else [old.reshape(batch, n_past, -1).astype(BF16), new_b]
        if lp > n_keys:
            parts.append(jnp.zeros((batch, lp - n_keys, width), BF16))
        return parts[0] if len(parts) == 1 else jnp.concatenate(parts, axis=1)

    if past is None:
        k_all, v_all, ki_all = keys(akb, None, LANES), keys(avb, None, LANES), keys(kib, None, LANES)
        fk_all, fv_all = keys(fkb, None, 512), keys(fvb, None, 512)
        logf_all = logf
    else:
        p_k, p_v, p_ki, p_fk, p_fv, p_logf = past
        ki_dup = jnp.concatenate([p_ki, p_ki], axis=-1)
        k_all, v_all, ki_all = keys(akb, p_k, LANES), keys(avb, p_v, LANES), keys(kib, ki_dup, LANES)
        fk_all, fv_all = keys(fkb, p_fk, 512), keys(fvb, p_fv, 512)
        logf_all = jnp.concatenate([p_logf.astype(F32), logf], axis=1)

    lf = jnp.pad(logf_all, ((0, 0), (0, lp - n_keys), (0, 0)))
    lf_t = lf.reshape(batch, lp // LANES, LANES, FOX_HEADS).transpose(0, 1, 3, 2)
    d_t = _cumsum(lf_t, tri).transpose(0, 2, 1, 3).reshape(batch, FOX_HEADS, lp)
    npair = FOX_HEADS // 2
    dq_p = d_t[:, :, n_keys - t:n_keys].reshape(batch, npair, 2, t).transpose(0, 1, 3, 2)
    dk_p = d_t.reshape(batch, npair, 2, lp // kb, kb).transpose(0, 1, 3, 2, 4)

    topk = min(TOPK_MAX, n_keys // 4)
    oa = _dsa(aq, qi, misc, k_all, v_all, ki_all, tri, batch=batch, t=t, tq=cfg["tq_dsa"], kb=kb,
              n_keys=n_keys, q_off=q_off, topk=topk)
    of = _fox(fq, fk_all, fv_all, dq_p, dk_p, batch=batch, t=t, tq=cfg["tq_fox"], kb=kb,
              n_keys=n_keys, q_off=q_off)
    h2, xn2 = _post(x2, oa, of, w["woa"], w["wof"], w["g_mem"], w["wq"], mk_b, mv_b, w["wo"], w["g_ffn"],
                    batch=batch, t=t, tb=cfg["tb_post"])
    eidx, gate = _peer_sel(xn2, w["wq_t"], w["keys1"], w["keys2"], cfg["tb_sel"])
    y = _peer_ffn(eidx, gate, xn2, h2, fg, w["uv"], cfg["tt_ffn"])
    return y.reshape(batch, t, D_MODEL), new_rows


def _config(t, n_tokens, has_past):
    if has_past:
        return dict(tb_in=n_tokens, kb=384, tq_dsa=t, tq_fox=t, tb_post=t, tb_sel=n_tokens, tt_ffn=min(128, n_tokens))
    return dict(tb_in=min(512, t), kb=min(512, t), tq_dsa=min(128, t), tq_fox=min(256, t), tb_post=min(256, t),
                tb_sel=min(256, n_tokens), tt_ffn=min(256, n_tokens))


def kernel(x_prompt, x_sample, cache_dsa_k, cache_dsa_v, cache_dsa_kidx, cache_fox_k, cache_fox_v, cache_fox_logf, cache_mem_k, cache_mem_v, mem_prompt, norm_mix_g, w_in, b_fox_f, w_out, mem_norm_g, w_mem_kv, norm_mem_g, w_mem_q, w_mem_o, norm_ffn_g, peer_w_q, peer_keys1, peer_keys2, peer_u, peer_v, final_norm_g):
    depth = w_in.shape[0]
    assert depth == 1, "single-layer trunk"
    l = 0
    bp, tp, _ = x_prompt.shape
    bs, ts, _ = x_sample.shape
    past_len = cache_dsa_k.shape[2]
    w = _prep_weights(norm_mix_g[l], w_in[l], b_fox_f[l], w_out[l], norm_mem_g[l], w_mem_q[l], w_mem_o[l],
                      norm_ffn_g[l], peer_w_q[l], peer_keys1[l], peer_keys2[l], peer_u[l], peer_v[l])
    fg = final_norm_g.astype(F32)[None, :]
    ii = np.arange(LANES)
    tri = jnp.asarray(ii[:, None] <= ii[None, :], BF16)

    mk, mv, mkb, mvb = _memkv(mem_prompt.reshape(bp * N_MEM, D_MODEL), mem_norm_g[l].astype(F32)[None, :],
                              w_mem_kv[l].astype(BF16))
    y_p, new_p = _layer(x_prompt, 0, None, mkb.reshape(bp, N_MEM, D_MODEL), mvb.reshape(bp, N_MEM, D_MODEL),
                        w, fg, tri, _config(tp, bp * tp, False))
    past = (cache_dsa_k[l], cache_dsa_v[l], cache_dsa_kidx[l], cache_fox_k[l], cache_fox_v[l], cache_fox_logf[l])
    y_s, new_s = _layer(x_sample, past_len, past, cache_mem_k[l].reshape(bs, N_MEM, D_MODEL).astype(BF16),
                        cache_mem_v[l].reshape(bs, N_MEM, D_MODEL).astype(BF16), w, fg, tri,
                        _config(ts, bs * ts, True))
    st = lambda a: a[None]
    mem_shape = (bp, N_MEM, MEM_HEADS, MEM_HEAD_DIM)
    return (y_p, y_s) + tuple(st(a) for a in new_p) + (st(mk.reshape(mem_shape)), st(mv.reshape(mem_shape))) \
        + tuple(st(a) for a in new_s)
```

```python
import functools

import jax
import jax.numpy as jnp
import numpy as np
from jax import lax
from jax.experimental import pallas as pl
from jax.experimental.pallas import tpu as pltpu

D_MODEL = 1024
CHUNK_SHIFT = 6
DSA_HEADS = 8
FOX_HEADS = 8
HEAD_DIM = 64
DSA_KV_HEADS = 2
IDX_HEADS = 4
IDX_DIM = 64
TOPK_MAX = 256
ROPE_THETA = 500000.0
ROT_DIM = HEAD_DIM // 4
N_MEM = 256
MEM_HEADS = 4
MEM_HEAD_DIM = D_MODEL // MEM_HEADS
PEER_HEADS = 8
PEER_KEY_DIM = 256
N_KEYS = 128
PEER_TOPK = 16
EPS = 1e-6
LANES = 128
VMEM_LIMIT = 48 * 1024 * 1024

F32 = jnp.float32
BF16 = jnp.bfloat16
I32 = jnp.int32
I16 = jnp.int16

HALF16 = 1 << 15
_NEG_INF_BITS = int(np.array(-np.inf, np.float32).view(np.int32))
KEY_NEG_INF = _NEG_INF_BITS ^ 0x7FFFFFFF
KEY_NEG_INF = KEY_NEG_INF - (1 << 32) if KEY_NEG_INF >= (1 << 31) else KEY_NEG_INF
MASK_VALUE = -1e30
MASK_BITS = int(np.array(MASK_VALUE, np.float32).view(np.int32))

SEG_Q = (0, 512)
SEG_K = (512, 640)
SEG_V = (640, 768)
SEG_QI = (768, 1024)
SEG_KI = (1024, 1152)
SEG_MISC = (1152, 1280)
SEG_FQ = (1280, 1792)
SEG_FK = (1792, 2304)
SEG_FV = (2304, 2816)
IN_COLS = 2816

CAND = [(a, b) for a in range(PEER_TOPK) for b in range(PEER_TOPK) if (a + 1) * (b + 1) <= PEER_TOPK]
N_CAND = len(CAND)


def _dot(a, b):
    return jnp.dot(a, b, preferred_element_type=F32)


def _dot_t(a, b):
    return lax.dot_general(a, b, (((1,), (1,)), ((), ())), preferred_element_type=F32)


def _rms(x, g):
    return x * lax.rsqrt(jnp.mean(x * x, axis=-1, keepdims=True) + EPS) * g


def _inproj_kernel(x_ref, g_ref, w_ref, b_ref, cos_ref, sa_ref, sb_ref,
                   aq_ref, ak_ref, av_ref, qi_ref, ki_ref, misc_ref, fq_ref, fk_ref, fv_ref,
                   akb_ref, avb_ref, kib_ref, fkb_ref, fvb_ref):
    xn = _rms(x_ref[...], g_ref[...]).astype(BF16)
    cos, sa, sb = cos_ref[...], sa_ref[...], sb_ref[...]

    def proj(seg):
        return _dot(xn, w_ref[:, seg[0]:seg[1]])

    def rope(y):
        outs = []
        for j in range(y.shape[1] // LANES):
            s = y[:, j * LANES:(j + 1) * LANES]
            outs.append(s * cos + pltpu.roll(s, LANES - ROT_DIM // 2, 1) * sa
                        + pltpu.roll(s, ROT_DIM // 2, 1) * sb)
        return outs[0] if len(outs) == 1 else jnp.concatenate(outs, axis=1)

    scale = HEAD_DIM ** -0.5
    aq_ref[...] = (rope(proj(SEG_Q)) * scale).astype(BF16)
    ak = rope(proj(SEG_K))
    ak_ref[...] = ak
    akb_ref[...] = ak.astype(BF16)
    av = proj(SEG_V)
    av_ref[...] = av
    avb_ref[...] = av.astype(BF16)
    qi_ref[...] = rope(proj(SEG_QI)).astype(BF16)
    ki = rope(proj(SEG_KI))
    ki_ref[...] = ki
    kib_ref[...] = ki.astype(BF16)
    z = proj(SEG_MISC)
    zb = z + b_ref[...]
    logf = jnp.minimum(zb, 0.0) - jnp.log1p(jnp.exp(-jnp.abs(zb)))
    lane = lax.broadcasted_iota(I32, z.shape, 1)
    misc_ref[...] = jnp.where((lane >= IDX_HEADS) & (lane < IDX_HEADS + FOX_HEADS), logf, z)
    fq_ref[...] = (proj(SEG_FQ) * scale).astype(BF16)
    fk = proj(SEG_FK)
    fk_ref[...] = fk
    fkb_ref[...] = fk.astype(BF16)
    fv = proj(SEG_FV)
    fv_ref[...] = fv
    fvb_ref[...] = fv.astype(BF16)


def _inproj(x2, g, w_cat, b_misc, cos, sa, sb, tb):
    n = x2.shape[0]
    ntab = cos.shape[0] // tb
    row = lambda w: pl.BlockSpec((tb, w), lambda i: (i, 0))
    tab = pl.BlockSpec((tb, LANES), lambda i: (i % ntab, 0))
    const = lambda s: pl.BlockSpec(s, lambda i: (0, 0))
    widths = [(512, BF16), (128, F32), (128, F32), (256, BF16), (128, F32), (128, F32),
              (512, BF16), (512, F32), (512, F32),
              (128, BF16), (128, BF16), (128, BF16), (512, BF16), (512, BF16)]
    return pl.pallas_call(
        _inproj_kernel,
        grid=(n // tb,),
        in_specs=[row(D_MODEL), const((1, D_MODEL)), const((D_MODEL, IN_COLS)), const((1, LANES)), tab, tab, tab],
        out_specs=[row(w) for w, _ in widths],
        out_shape=[jax.ShapeDtypeStruct((n, w), dt) for w, dt in widths],
        compiler_params=pltpu.CompilerParams(dimension_semantics=("arbitrary",), vmem_limit_bytes=VMEM_LIMIT),
        name="inproj",
    )(x2, g, w_cat, b_misc, cos, sa, sb)


def _split3(x):
    hi = x.astype(BF16)
    r1 = x - hi.astype(F32)
    mid = r1.astype(BF16)
    lo = (r1 - mid.astype(F32)).astype(BF16)
    return hi, mid, lo


def _cumsum_kernel(x_ref, tri_ref, o_ref):
    tri = tri_ref[...]

    def body(c, carry):
        hi, mid, lo = _split3(x_ref[0, c])
        out = _dot(hi, tri) + _dot(mid, tri) + _dot(lo, tri) + carry
        o_ref[0, c] = out
        return out[:, LANES - 1:LANES]

    lax.fori_loop(0, x_ref.shape[1], body, jnp.zeros((FOX_HEADS, 1), F32))


def _cumsum(logf_t, tri):
    b, nchunk = logf_t.shape[:2]
    spec = pl.BlockSpec((1, nchunk, FOX_HEADS, LANES), lambda i: (i, 0, 0, 0))
    return pl.pallas_call(
        _cumsum_kernel,
        grid=(b,),
        in_specs=[spec, pl.BlockSpec((LANES, LANES), lambda i: (0, 0))],
        out_specs=spec,
        out_shape=jax.ShapeDtypeStruct(logf_t.shape, F32),
        name="cumsum",
    )(logf_t, tri)


def _dsa_kernel(aq_ref, qi_ref, misc_ref, k_ref, v_ref, ki_ref, tri_ref, o_ref, s_ref, s16_ref,
                *, tq, kb, n_keys, q_off, topk):
    i = pl.program_id(1)
    sub = kb // LANES
    q_last = q_off + (i + 1) * tq - 1
    adm_len = jnp.minimum(n_keys, ((q_last >> CHUNK_SHIFT) + 1) << CHUNK_SHIFT)
    nkb = (adm_len + kb - 1) // kb
    qchunk = (q_off + i * tq + lax.broadcasted_iota(I32, (tq, 1), 0)) >> CHUNK_SHIFT
    lane = lax.broadcasted_iota(I32, (tq, LANES), 1)
    lo_half = lane < HEAD_DIM

    zero = jnp.zeros((), BF16)
    qh = []
    for j in range(IDX_HEADS // 2):
        slab = qi_ref[:, j * LANES:(j + 1) * LANES]
        qh += [jnp.where(lo_half, slab, zero), jnp.where(lo_half, zero, slab)]
    wi = [misc_ref[:, h:h + 1] for h in range(IDX_HEADS)]

    def p1(c, _):
        off = pl.multiple_of(c * kb, kb)
        kib = ki_ref[0, pl.ds(off, kb), :]
        sc = jnp.zeros((tq, kb), F32)
        for h in range(IDX_HEADS):
            sc = sc + jnp.maximum(_dot_t(qh[h], kib), 0.0) * wi[h]
        kpos = off + lax.broadcasted_iota(I32, (1, kb), 1)
        ok = ((kpos >> CHUNK_SHIFT) <= qchunk) & (kpos < n_keys)
        bits = lax.bitcast_convert_type(sc, I32)
        key = jnp.where(ok, bits ^ ((bits >> 31) & 0x7FFFFFFF), KEY_NEG_INF)
        for j in range(sub):
            piece = key[:, j * LANES:(j + 1) * LANES]
            s_ref[c * sub + j] = piece
            s16_ref[c * sub + j] = (piece >> 16).astype(I16)
        return 0

    lax.fori_loop(0, nkb, p1, 0)

    def count_ge(trial):
        tb = jnp.broadcast_to(trial, (tq, LANES))

        def body(c, acc):
            for j in range(sub):
                acc = acc + (s_ref[c * sub + j] >= tb).astype(I32)
            return acc

        acc = lax.fori_loop(0, nkb, body, jnp.zeros((tq, LANES), I32))
        return jnp.sum(acc, axis=1, keepdims=True)

    def count16_ge(trial):
        tb = jnp.broadcast_to(trial.astype(I16), (tq, LANES))

        def body(c, acc):
            for j in range(sub):
                acc = acc + (s16_ref[c * sub + j] >= tb).astype(I16)
            return acc

        acc = lax.fori_loop(0, nkb, body, jnp.zeros((tq, LANES), I16))
        return jnp.sum(acc.astype(I32), axis=1, keepdims=True)

    def bisect16(need_cnt):
        def one(it, cand):
            trial = cand | (jnp.int32(1) << (15 - it))
            return jnp.where(count16_ge(trial - HALF16) >= need_cnt, trial, cand)

        return lax.fori_loop(0, 16, one, jnp.zeros((tq, 1), I32))

    hi_star = bisect16(topk) - HALF16
    above = jnp.where(hi_star == HALF16 - 1, 0, count16_ge(jnp.minimum(hi_star + 1, HALF16 - 1)))
    hb = jnp.broadcast_to(hi_star.astype(I16), (tq, LANES))

    def to_low(c, _):
        for j in range(sub):
            low = ((s_ref[c * sub + j] & 0xFFFF) - HALF16).astype(I16)
            s16_ref[c * sub + j] = jnp.where(s16_ref[c * sub + j] == hb, low, -HALF16)
        return 0

    lax.fori_loop(0, nkb, to_low, 0)
    vstar = (hi_star << 16) | bisect16(topk - above)
    need = (topk - count_ge(vstar + 1)).astype(F32)

    vb = jnp.broadcast_to(vstar, (tq, LANES))
    tri = tri_ref[...]

    def p2b(c, carry):
        for j in range(sub):
            blk = s_ref[c * sub + j]
            eq = blk == vb
            pref = _dot(jnp.where(eq, 1.0, 0.0).astype(BF16), tri)
            sel = ((blk > vb) | (eq & (pref + carry <= need))) & (blk > KEY_NEG_INF)
            s_ref[c * sub + j] = jnp.where(sel, 0, MASK_BITS)
            carry = carry + pref[:, LANES - 1:LANES]
        return carry

    lax.fori_loop(0, nkb, p2b, jnp.zeros((tq, 1), F32))

    rep = DSA_HEADS // DSA_KV_HEADS
    res = []
    for g in range(DSA_KV_HEADS):
        in_g = lo_half if g == 0 else jnp.logical_not(lo_half)
        q_all = jnp.concatenate(
            [jnp.where(in_g, aq_ref[:, r * LANES:(r + 1) * LANES], zero) for r in range(rep)], axis=0)

        in_g_k = lax.broadcasted_iota(I32, (kb, LANES), 1) < HEAD_DIM
        if g == 1:
            in_g_k = jnp.logical_not(in_g_k)

        def p3(c, carry):
            m, acc = carry
            off = pl.multiple_of(c * kb, kb)
            lg = _dot_t(q_all, k_ref[0, pl.ds(off, kb), :])
            bias = jnp.concatenate(
                [lax.bitcast_convert_type(s_ref[c * sub + j], F32) for j in range(sub)], axis=1)
            lg = lg + jnp.concatenate([bias] * rep, axis=0)
            mn = jnp.maximum(m, jnp.max(lg, axis=1, keepdims=True))
            a = jnp.exp(m - mn)
            p = jnp.exp(lg - mn)
            vone = jnp.where(in_g_k, v_ref[0, pl.ds(off, kb), :], jnp.ones((), BF16))
            acc = a * acc + _dot(p.astype(BF16), vone)
            return mn, acc

        init = (jnp.full((rep * tq, 1), MASK_VALUE, F32), jnp.zeros((rep * tq, LANES), F32))
        _, acc = lax.fori_loop(0, nkb, p3, init)
        denom = acc[:, HEAD_DIM * (1 - g):HEAD_DIM * (1 - g) + 1]
        res.append(acc / denom)
    for r in range(rep):
        o_ref[:, r * LANES:(r + 1) * LANES] = jnp.where(
            lo_half, res[0][r * tq:(r + 1) * tq], res[1][r * tq:(r + 1) * tq]).astype(BF16)


def _dsa(aq, qi, misc, k_b, v_b, ki_b, tri, *, batch, t, tq, kb, n_keys, q_off, topk):
    lp = k_b.shape[1]
    nq = t // tq
    qrow = lambda w: pl.BlockSpec((tq, w), lambda b, i: (b * nq + i, 0))
    keys = pl.BlockSpec((1, lp, LANES), lambda b, i: (b, 0, 0))
    kern = functools.partial(_dsa_kernel, tq=tq, kb=kb, n_keys=n_keys, q_off=q_off, topk=topk)
    return pl.pallas_call(
        kern,
        grid=(batch, nq),
        in_specs=[qrow(512), qrow(256), qrow(LANES), keys, keys, keys,
                  pl.BlockSpec((LANES, LANES), lambda b, i: (0, 0))],
        out_specs=qrow(512),
        out_shape=jax.ShapeDtypeStruct((batch * t, 512), BF16),
        scratch_shapes=[pltpu.VMEM((lp // LANES, tq, LANES), I32), pltpu.VMEM((lp // LANES, tq, LANES), I16)],
        compiler_params=pltpu.CompilerParams(dimension_semantics=("arbitrary", "arbitrary"),
                                             vmem_limit_bytes=VMEM_LIMIT),
        name="dsa",
    )(aq, qi, misc, k_b, v_b, ki_b, tri)


def _fox_kernel(q_ref, k_ref, v_ref, dq_ref, dk_ref, o_ref, *, tq, kb, n_keys, q_off):
    i = pl.program_id(2)
    nkb = (jnp.minimum(n_keys, q_off + (i + 1) * tq) + kb - 1) // kb
    lane = lax.broadcasted_iota(I32, (tq, LANES), 1)
    lo_half = lane < HEAD_DIM
    zero = jnp.zeros((), BF16)
    q = q_ref[...]
    q_all = jnp.concatenate([jnp.where(lo_half, q, zero), jnp.where(lo_half, zero, q)], axis=0)
    dq = dq_ref[0, 0]
    qpos = q_off + i * tq + lax.broadcasted_iota(I32, (tq, 1), 0)

    def body(c, carry, masked):
        off = pl.multiple_of(c * kb, kb)
        lg = _dot_t(q_all, k_ref[0, pl.ds(off, kb), :])
        vblk = v_ref[0, pl.ds(off, kb), :]
        dk = dk_ref[0, 0, c]
        if masked:
            kpos = off + lax.broadcasted_iota(I32, (1, kb), 1)
            ok = (kpos <= qpos) & (kpos < n_keys)
        out = []
        for e in range(2):
            m, acc = carry[e]
            lge = lg[e * tq:(e + 1) * tq] + dq[:, e:e + 1] - dk[e:e + 1, :]
            if masked:
                lge = jnp.where(ok, lge, MASK_VALUE)
            mn = jnp.maximum(m, jnp.max(lge, axis=1, keepdims=True))
            a = jnp.exp(m - mn)
            p = jnp.exp(lge - mn)
            vone = jnp.where(lo_k if e == 0 else jnp.logical_not(lo_k), vblk, jnp.ones((), BF16))
            acc = a * acc + _dot(p.astype(BF16), vone)
            out.append((mn, acc))
        return tuple(out)

    lo_k = lax.broadcasted_iota(I32, (kb, LANES), 1) < HEAD_DIM
    one = (jnp.full((tq, 1), MASK_VALUE, F32), jnp.zeros((tq, LANES), F32))
    n_full = jnp.minimum((q_off + i * tq + 1) // kb, n_keys // kb)
    carry = lax.fori_loop(0, n_full, functools.partial(body, masked=False), (one, one))
    (_, a0), (_, a1) = lax.fori_loop(n_full, nkb, functools.partial(body, masked=True), carry)
    o_ref[...] = jnp.where(lo_half, a0 / a0[:, HEAD_DIM:HEAD_DIM + 1], a1 / a1[:, 0:1]).astype(BF16)


def _fox(fq, k_b, v_b, dq_p, dk_p, *, batch, t, tq, kb, n_keys, q_off):
    lp = k_b.shape[1]
    nq = t // tq
    npair = FOX_HEADS // 2
    kern = functools.partial(_fox_kernel, tq=tq, kb=kb, n_keys=n_keys, q_off=q_off)
    kv = pl.BlockSpec((1, lp, LANES), lambda b, p, i: (b, 0, p))
    qo = pl.BlockSpec((tq, LANES), lambda b, p, i: (b * nq + i, p))
    return pl.pallas_call(
        kern,
        grid=(batch, npair, nq),
        in_specs=[qo, kv, kv,
                  pl.BlockSpec((1, 1, tq, 2), lambda b, p, i: (b, p, i, 0)),
                  pl.BlockSpec((1, 1, lp // kb, 2, kb), lambda b, p, i: (b, p, 0, 0, 0))],
        out_specs=qo,
        out_shape=jax.ShapeDtypeStruct((batch * t, FOX_HEADS * HEAD_DIM), BF16),
        compiler_params=pltpu.CompilerParams(dimension_semantics=("arbitrary",) * 3,
                                             vmem_limit_bytes=VMEM_LIMIT),
        name="fox",
    )(fq, k_b, v_b, dq_p, dk_p)


def _memkv_kernel(x_ref, g_ref, w_ref, k_ref, v_ref, kb_ref, vb_ref):
    xn = _rms(x_ref[...], g_ref[...]).astype(BF16)
    k = _dot(xn, w_ref[:, :D_MODEL])
    v = _dot(xn, w_ref[:, D_MODEL:])
    k_ref[...] = k
    v_ref[...] = v
    kb_ref[...] = k.astype(BF16)
    vb_ref[...] = v.astype(BF16)


def _memkv(mem2, g, w_kv):
    n = mem2.shape[0]
    row = pl.BlockSpec((N_MEM, D_MODEL), lambda i: (i, 0))
    return pl.pallas_call(
        _memkv_kernel,
        grid=(n // N_MEM,),
        in_specs=[row, pl.BlockSpec((1, D_MODEL), lambda i: (0, 0)),
                  pl.BlockSpec((D_MODEL, 2 * D_MODEL), lambda i: (0, 0))],
        out_specs=[row] * 4,
        out_shape=[jax.ShapeDtypeStruct((n, D_MODEL), dt) for dt in (F32, F32, BF16, BF16)],
        compiler_params=pltpu.CompilerParams(dimension_semantics=("arbitrary",), vmem_limit_bytes=VMEM_LIMIT),
        name="memkv",
    )(mem2, g, w_kv)


def _post_kernel(x_ref, oa_ref, of_ref, woa_ref, wof_ref, gm_ref, wq_ref, mk_ref, mv_ref, wo_ref, gf_ref,
                 h_ref, xn_ref):
    h = x_ref[...] + _dot(oa_ref[...], woa_ref[...]) + _dot(of_ref[...], wof_ref[...])
    hn = _rms(h, gm_ref[...]).astype(BF16)
    q = (_dot(hn, wq_ref[...]) * (MEM_HEAD_DIM ** -0.5)).astype(BF16)
    outs = []
    for hd in range(MEM_HEADS):
        cs = slice(hd * MEM_HEAD_DIM, (hd + 1) * MEM_HEAD_DIM)
        lg = _dot_t(q[:, cs], mk_ref[0, :, cs])
        p = jnp.exp(lg - jnp.max(lg, axis=1, keepdims=True))
        p = p / jnp.sum(p, axis=1, keepdims=True)
        outs.append(_dot(p.astype(BF16), mv_ref[0, :, cs]))
    o = jnp.concatenate(outs, axis=1).astype(BF16)
    h = h + _dot(o, wo_ref[...])
    h_ref[...] = h
    xn_ref[...] = _rms(h, gf_ref[...])


def _post(x2, oa, of, woa, wof, gm, wq, mk_b, mv_b, wo, gf, *, batch, t, tb):
    nt = t // tb
    row = lambda w: pl.BlockSpec((tb, w), lambda b, i: (b * nt + i, 0))
    const = lambda s: pl.BlockSpec(s, lambda b, i: (0, 0))
    mem = pl.BlockSpec((1, N_MEM, D_MODEL), lambda b, i: (b, 0, 0))
    return pl.pallas_call(
        _post_kernel,
        grid=(batch, nt),
        in_specs=[row(D_MODEL), row(512), row(512), const((512, D_MODEL)), const((512, D_MODEL)),
                  const((1, D_MODEL)), const((D_MODEL, D_MODEL)), mem, mem, const((D_MODEL, D_MODEL)),
                  const((1, D_MODEL))],
        out_specs=[row(D_MODEL), row(D_MODEL)],
        out_shape=[jax.ShapeDtypeStruct((batch * t, D_MODEL), F32)] * 2,
        compiler_params=pltpu.CompilerParams(dimension_semantics=("arbitrary", "arbitrary"),
                                             vmem_limit_bytes=VMEM_LIMIT),
        name="post",
    )(x2, oa, of, woa, wof, gm, wq, mk_b, mv_b, wo, gf)


def _top16(s):
    r = s.shape[0]
    row = lax.broadcasted_iota(I32, s.shape, 0).astype(F32)
    vals, ids = [], []
    for _ in range(PEER_TOPK):
        m = jnp.max(s, axis=0, keepdims=True)
        am = jnp.min(jnp.where(s == m, row, float(r)), axis=0, keepdims=True)
        vals.append(m)
        ids.append(am)
        s = jnp.where(row == am, -jnp.inf, s)
    return jnp.concatenate(vals, axis=0), jnp.concatenate(ids, axis=0)


def _peer_sel_kernel(x_ref, wqt_ref, k1_ref, k2_ref, eidx_ref, gate_ref, qt_scr, e_scr, g_scr):
    half = PEER_KEY_DIM // 2
    qt_scr[...] = _dot_t(wqt_ref[...], x_ref[...].astype(BF16)).astype(BF16)
    n = qt_scr.shape[1]
    pad = (-N_CAND) % 8

    def head(h, _):
        r0 = pl.multiple_of(h * PEER_KEY_DIM, PEER_KEY_DIM)
        v1, i1 = _top16(_dot(k1_ref[h], qt_scr[pl.ds(r0, half), :]))
        v2, i2 = _top16(_dot(k2_ref[h], qt_scr[pl.ds(r0 + half, half), :]))
        cand = jnp.concatenate([v1[a:a + 1] + v2[b:b + 1] for a, b in CAND]
                               + [jnp.full((pad, n), -jnp.inf, F32)], axis=0)
        cidx = jnp.concatenate([i1[a:a + 1] * float(N_KEYS) + i2[b:b + 1] for a, b in CAND]
                               + [jnp.zeros((pad, n), F32)], axis=0)
        sv, sr = _top16(cand)
        row = lax.broadcasted_iota(I32, cand.shape, 0).astype(F32)
        eid = jnp.concatenate(
            [jnp.sum(jnp.where(row == sr[k:k + 1], cidx, 0.0), axis=0, keepdims=True) for k in range(PEER_TOPK)],
            axis=0)
        p = jnp.exp(sv - sv[0:1])
        o0 = pl.multiple_of(h * PEER_TOPK, PEER_TOPK)
        e_scr[pl.ds(o0, PEER_TOPK), :] = eid
        g_scr[pl.ds(o0, PEER_TOPK), :] = p / jnp.sum(p, axis=0, keepdims=True)
        return 0

    lax.fori_loop(0, PEER_HEADS, head, 0)
    eidx_ref[...] = e_scr[...].T.astype(I32)
    gate_ref[...] = g_scr[...].T


def _peer_sel(xn2, wq_t, keys1, keys2, tb):
    n = xn2.shape[0]
    kspec = pl.BlockSpec((PEER_HEADS, N_KEYS, PEER_KEY_DIM // 2), lambda i: (0, 0, 0))
    out = pl.BlockSpec((tb, PEER_HEADS * PEER_TOPK), lambda i: (i, 0))
    return pl.pallas_call(
        _peer_sel_kernel,
        grid=(n // tb,),
        in_specs=[pl.BlockSpec((tb, D_MODEL), lambda i: (i, 0)),
                  pl.BlockSpec((PEER_HEADS * PEER_KEY_DIM, D_MODEL), lambda i: (0, 0)), kspec, kspec],
        out_specs=[out, out],
        out_shape=[jax.ShapeDtypeStruct((n, PEER_HEADS * PEER_TOPK), I32),
                   jax.ShapeDtypeStruct((n, PEER_HEADS * PEER_TOPK), F32)],
        scratch_shapes=[pltpu.VMEM((PEER_HEADS * PEER_KEY_DIM, tb), BF16)]
        + [pltpu.VMEM((PEER_HEADS * PEER_TOPK, tb), F32)] * 2,
        compiler_params=pltpu.CompilerParams(dimension_semantics=("arbitrary",), vmem_limit_bytes=VMEM_LIMIT),
        name="peer_sel",
    )(xn2, wq_t, keys1, keys2)


N_PICK = PEER_HEADS * PEER_TOPK
P_AHEAD = 6
N_SLOT = P_AHEAD + 4
GROUP = 8
HI_MASK = -65536


def _gelu(x):
    return 0.5 * x * (1.0 + lax.erf(x * (2.0 ** -0.5)))


def _peer_ffn_kernel(eidx_ref, gate_ref, x_ref, h_ref, fg_ref, ones_ref, eye_ref, uv_ref, y_ref,
                     buf, hd_scr, w_scr, wc_scr, acc, sem, *, tt):
    @pl.when(pl.program_id(0) == 0)
    def _():
        buf[...] = jnp.zeros(buf.shape, I32)
        hd_scr[...] = jnp.zeros(hd_scr.shape, F32)
        w_scr[...] = jnp.zeros(w_scr.shape, F32)
        wc_scr[...] = jnp.zeros(wc_scr.shape, F32)

    n_tiles = D_MODEL // LANES
    n_groups = N_PICK // GROUP
    c1_at, c2_at = 1, 10

    def expert_copy(t, j, slot):
        return pltpu.make_async_copy(uv_ref.at[eidx_ref[t * N_PICK + j]], buf.at[slot, :, j, :], sem.at[slot])

    def packed_rows(slot, rows):
        return jnp.concatenate([buf[slot, s, rows, :] for s in range(n_tiles)], axis=1)

    def lane_tiles_sum(a):
        out = a[:, 0:LANES]
        for c in range(1, n_tiles):
            out = out + a[:, c * LANES:(c + 1) * LANES]
        return out

    def step(t, r, issue, gather_u, mix1, mix2):
        slot_v = (r - 3) % N_SLOT
        par = r % 2
        if gather_u:
            for _ in range(N_PICK):
                pltpu.make_async_copy(uv_ref.at[0], buf.at[r, :, 0, :], sem.at[r]).wait()
            xrow = jnp.broadcast_to(x_ref[pl.ds(t, 1), :], (GROUP, D_MODEL))
        acc8 = jnp.zeros((GROUP, D_MODEL), F32)
        part = w_new = wcol_new = None
        for k in range(n_groups):
            rows = slice(k * GROUP, (k + 1) * GROUP)
            if issue:
                for e in range(GROUP):
                    expert_copy(t + P_AHEAD, k * GROUP + e, (r + P_AHEAD) % N_SLOT).start(priority=e % 2)
            if gather_u:
                if part is not None:
                    hd_scr[par, (k - 1) * GROUP:k * GROUP, :] = part
                u = lax.bitcast_convert_type(packed_rows(r, rows) & HI_MASK, F32)
                part = lane_tiles_sum(u * xrow)
            v = lax.bitcast_convert_type(packed_rows(slot_v, rows) << 16, F32)
            acc8 = acc8 + v * jnp.concatenate([wc_scr[rows, :]] * n_tiles, axis=1)
            if mix1 and k == c1_at:
                hdp = hd_scr[1 - par]
                hi = hdp.astype(BF16)
                lo = (hdp - hi.astype(F32)).astype(BF16)
                ones8 = ones_ref[0:GROUP, :]
                hd = _dot_t(ones8, hi) + _dot_t(ones8, lo)
                w_new = gate_ref[pl.ds(jnp.maximum(t - 1, 0), 1), :] * _gelu(hd)
            if mix2 and k == c2_at:
                diag = (eye_ref[...] * jnp.broadcast_to(w_scr[0:1, :], (LANES, LANES))).astype(BF16)
                wcol_new = _dot(diag, ones_ref[...])
        if gather_u:
            hd_scr[par, (n_groups - 1) * GROUP:, :] = part
        if mix2:
            wc_scr[...] = wcol_new
        if mix1:
            w_scr[...] = w_new
        acc[pl.ds(jnp.maximum(t - 3, 0), 1), :] = jnp.sum(acc8, axis=0, keepdims=True)

    for t0 in range(P_AHEAD):
        for j in range(N_PICK):
            expert_copy(t0, j, t0).start(priority=j % 2)

    n_round = (tt - P_AHEAD) // N_SLOT

    def one_round(q, _):
        for r in range(N_SLOT):
            step(q * N_SLOT + r, r, True, True, True, True)
        return 0

    lax.fori_loop(0, n_round, one_round, 0)
    for t in range(n_round * N_SLOT, tt + 3):
        step(t, t % N_SLOT, t + P_AHEAD < tt, t < tt, t <= tt, t <= tt + 1)
    y_ref[...] = _rms(h_ref[...] + acc[...], fg_ref[...])


def _peer_ffn(eidx, gate, xn2, h2, fg, uv, tt):
    n = xn2.shape[0]
    row = lambda w: pl.BlockSpec((tt, w), lambda i: (i, 0))
    const = lambda s: pl.BlockSpec(s, lambda i: (0, 0))
    return pl.pallas_call(
        functools.partial(_peer_ffn_kernel, tt=tt),
        grid=(n // tt,),
        in_specs=[pl.BlockSpec((tt * N_PICK,), lambda i: (i,), memory_space=pltpu.SMEM),
                  row(N_PICK), row(D_MODEL), row(D_MODEL), const((1, D_MODEL)), const((LANES, LANES)),
                  const((LANES, LANES)), pl.BlockSpec(memory_space=pl.ANY)],
        out_specs=row(D_MODEL),
        out_shape=jax.ShapeDtypeStruct((n, D_MODEL), F32),
        scratch_shapes=[pltpu.VMEM((N_SLOT, D_MODEL // LANES, N_PICK, LANES), I32), pltpu.VMEM((2, N_PICK, LANES), F32),
                        pltpu.VMEM((GROUP, LANES), F32), pltpu.VMEM((N_PICK, LANES), F32),
                        pltpu.VMEM((tt, D_MODEL), F32), pltpu.SemaphoreType.DMA((N_SLOT,))],
        compiler_params=pltpu.CompilerParams(dimension_semantics=("arbitrary",), vmem_limit_bytes=VMEM_LIMIT),
        name="peer_ffn",
    )(eidx.reshape(-1), gate, xn2, h2, fg, jnp.ones((LANES, LANES), BF16), jnp.eye(LANES, dtype=F32), uv)


def _rope_tables(pos):
    inv_freq = ROPE_THETA ** (-jnp.arange(0, ROT_DIM, 2, dtype=F32) / ROT_DIM)
    ang = pos.astype(F32)[:, None] * inv_freq[None, :]
    cos, sin = jnp.cos(ang), jnp.sin(ang)
    n = pos.shape[0]
    rest = HEAD_DIM - ROT_DIM
    hr = ROT_DIM // 2
    c = jnp.concatenate([cos, cos, jnp.ones((n, rest), F32)], axis=1)
    a = jnp.concatenate([-sin, jnp.zeros((n, hr + rest), F32)], axis=1)
    b = jnp.concatenate([jnp.zeros((n, hr), F32), sin, jnp.zeros((n, rest), F32)], axis=1)
    return tuple(jnp.tile(z, (1, LANES // HEAD_DIM)) for z in (c, a, b))


Q_PERM = np.concatenate([np.r_[j * HEAD_DIM:(j + 1) * HEAD_DIM, (4 + j) * HEAD_DIM:(5 + j) * HEAD_DIM]
                         for j in range(4)])


def _pack_uv(u, v):
    bits = lambda a: lax.bitcast_convert_type(a.astype(jnp.bfloat16), jnp.uint16).astype(jnp.uint32)
    return lax.bitcast_convert_type((bits(u) << 16) | bits(v), I32).reshape(-1, D_MODEL // LANES, LANES)


def _prep_weights(norm_mix_g, w_in, b_fox_f, w_out, norm_mem_g, w_mem_q, w_mem_o, norm_ffn_g, peer_w_q,
                  peer_keys1, peer_keys2, peer_u, peer_v):
    sp = np.cumsum([0, 512, 128, 128, 256, 64, 4, 512, 512, 512, 8])
    a_q, a_k, a_v, a_qi, a_ki, a_w, f_q, f_k, f_v, f_f = [w_in[:, sp[i]:sp[i + 1]] for i in range(10)]
    zpad = jnp.zeros((D_MODEL, LANES - IDX_HEADS - FOX_HEADS), F32)
    w_cat = jnp.concatenate([a_q[:, Q_PERM], a_k, a_v, a_qi, a_ki, a_ki, a_w, f_f, zpad, f_q, f_k, f_v],
                            axis=1).astype(BF16)
    b_misc = jnp.concatenate([jnp.zeros((IDX_HEADS,), F32), b_fox_f.astype(F32),
                              jnp.zeros((LANES - IDX_HEADS - FOX_HEADS,), F32)])[None, :]
    r = lambda g: g.astype(F32)[None, :]
    return dict(
        g_mix=r(norm_mix_g), w_cat=w_cat, b_misc=b_misc,
        woa=w_out[:512][Q_PERM].astype(BF16), wof=w_out[512:].astype(BF16),
        g_mem=r(norm_mem_g), wq=w_mem_q.astype(BF16), wo=w_mem_o.astype(BF16), g_ffn=r(norm_ffn_g),
        wq_t=peer_w_q.T.astype(BF16), keys1=peer_keys1.astype(BF16), keys2=peer_keys2.astype(BF16),
        uv=_pack_uv(peer_u, peer_v),
    )


def _round_up(x, m):
    return (x + m - 1) // m * m


def _layer(x, q_off, past, mk_b, mv_b, w, fg, tri, cfg):
    batch, t, _ = x.shape
    n = batch * t
    x2 = x.reshape(n, D_MODEL)
    tabs = _rope_tables(q_off + jnp.arange(t, dtype=I32))
    tb_in = cfg["tb_in"]
    if tb_in > t:
        tabs = tuple(jnp.tile(z, (tb_in // t, 1)) for z in tabs)
    (aq, ak, av, qi, ki, misc, fq, fk, fv, akb, avb, kib, fkb, fvb) = _inproj(
        x2, w["g_mix"], w["w_cat"], w["b_misc"], *tabs, tb_in)
    logf = misc[:, IDX_HEADS:IDX_HEADS + FOX_HEADS].reshape(batch, t, FOX_HEADS)
    new_rows = (ak.reshape(batch, t, DSA_KV_HEADS, HEAD_DIM), av.reshape(batch, t, DSA_KV_HEADS, HEAD_DIM),
                ki[:, :IDX_DIM].reshape(batch, t, IDX_DIM), fk.reshape(batch, t, FOX_HEADS, HEAD_DIM),
                fv.reshape(batch, t, FOX_HEADS, HEAD_DIM), logf)

    kb = cfg["kb"]
    n_past = 0 if past is None else past[0].shape[1]
    n_keys = n_past + t
    lp = _round_up(n_keys, kb)

    def keys(new_b, old, width):
        new_b = new_b.reshape(batch, t, width)
        parts = [new_b] if old is None else [old.reshape(batch, n_past, -1).astype(BF16), new_b]
        if lp > n_keys:
            parts.append(jnp.zeros((batch, lp - n_keys, width), BF16))
        return parts[0] if len(parts) == 1 else jnp.concatenate(parts, axis=1)

    if past is None:
        k_all, v_all, ki_all = keys(akb, None, LANES), keys(avb, None, LANES), keys(kib, None, LANES)
        fk_all, fv_all = keys(fkb, None, 512), keys(fvb, None, 512)
        logf_all = logf
    else:
        p_k, p_v, p_ki, p_fk, p_fv, p_logf = past
        ki_dup = jnp.concatenate([p_ki, p_ki], axis=-1)
        k_all, v_all, ki_all = keys(akb, p_k, LANES), keys(avb, p_v, LANES), keys(kib, ki_dup, LANES)
        fk_all, fv_all = keys(fkb, p_fk, 512), keys(fvb, p_fv, 512)
        logf_all = jnp.concatenate([p_logf.astype(F32), logf], axis=1)

    lf = jnp.pad(logf_all, ((0, 0), (0, lp - n_keys), (0, 0)))
    lf_t = lf.reshape(batch, lp // LANES, LANES, FOX_HEADS).transpose(0, 1, 3, 2)
    d_t = _cumsum(lf_t, tri).transpose(0, 2, 1, 3).reshape(batch, FOX_HEADS, lp)
    npair = FOX_HEADS // 2
    dq_p = d_t[:, :, n_keys - t:n_keys].reshape(batch, npair, 2, t).transpose(0, 1, 3, 2)
    dk_p = d_t.reshape(batch, npair, 2, lp // kb, kb).transpose(0, 1, 3, 2, 4)

    topk = min(TOPK_MAX, n_keys // 4)
    oa = _dsa(aq, qi, misc, k_all, v_all, ki_all, tri, batch=batch, t=t, tq=cfg["tq_dsa"], kb=kb,
              n_keys=n_keys, q_off=q_off, topk=topk)
    of = _fox(fq, fk_all, fv_all, dq_p, dk_p, batch=batch, t=t, tq=cfg["tq_fox"], kb=kb,
              n_keys=n_keys, q_off=q_off)
    h2, xn2 = _post(x2, oa, of, w["woa"], w["wof"], w["g_mem"], w["wq"], mk_b, mv_b, w["wo"], w["g_ffn"],
                    batch=batch, t=t, tb=cfg["tb_post"])
    eidx, gate = _peer_sel(xn2, w["wq_t"], w["keys1"], w["keys2"], cfg["tb_sel"])
    y = _peer_ffn(eidx, gate, xn2, h2, fg, w["uv"], cfg["tt_ffn"])
    return y.reshape(batch, t, D_MODEL), new_rows


def _config(t, n_tokens, has_past):
    if has_past:
        return dict(tb_in=n_tokens, kb=384, tq_dsa=t, tq_fox=t, tb_post=t, tb_sel=n_tokens, tt_ffn=min(128, n_tokens))
    return dict(tb_in=min(512, t), kb=min(512, t), tq_dsa=min(128, t), tq_fox=min(256, t), tb_post=min(256, t),
                tb_sel=min(256, n_tokens), tt_ffn=min(256, n_tokens))


def kernel(x_prompt, x_sample, cache_dsa_k, cache_dsa_v, cache_dsa_kidx, cache_fox_k, cache_fox_v, cache_fox_logf, cache_mem_k, cache_mem_v, mem_prompt, norm_mix_g, w_in, b_fox_f, w_out, mem_norm_g, w_mem_kv, norm_mem_g, w_mem_q, w_mem_o, norm_ffn_g, peer_w_q, peer_keys1, peer_keys2, peer_u, peer_v, final_norm_g):
    depth = w_in.shape[0]
    assert depth == 1, "single-layer trunk"
    l = 0
    bp, tp, _ = x_prompt.shape
    bs, ts, _ = x_sample.shape
    past_len = cache_dsa_k.shape[2]
    w = _prep_weights(norm_mix_g[l], w_in[l], b_fox_f[l], w_out[l], norm_mem_g[l], w_mem_q[l], w_mem_o[l],
                      norm_ffn_g[l], peer_w_q[l], peer_keys1[l], peer_keys2[l], peer_u[l], peer_v[l])
    fg = final_norm_g.astype(F32)[None, :]
    ii = np.arange(LANES)
    tri = jnp.asarray(ii[:, None] <= ii[None, :], BF16)

    mk, mv, mkb, mvb = _memkv(mem_prompt.reshape(bp * N_MEM, D_MODEL), mem_norm_g[l].astype(F32)[None, :],
                              w_mem_kv[l].astype(BF16))
    y_p, new_p = _layer(x_prompt, 0, None, mkb.reshape(bp, N_MEM, D_MODEL), mvb.reshape(bp, N_MEM, D_MODEL),
                        w, fg, tri, _config(tp, bp * tp, False))
    past = (cache_dsa_k[l], cache_dsa_v[l], cache_dsa_kidx[l], cache_fox_k[l], cache_fox_v[l], cache_fox_logf[l])
    y_s, new_s = _layer(x_sample, past_len, past, cache_mem_k[l].reshape(bs, N_MEM, D_MODEL).astype(BF16),
                        cache_mem_v[l].reshape(bs, N_MEM, D_MODEL).astype(BF16), w, fg, tri,
                        _config(ts, bs * ts, True))
    st = lambda a: a[None]
    mem_shape = (bp, N_MEM, MEM_HEADS, MEM_HEAD_DIM)
    return (y_p, y_s) + tuple(st(a) for a in new_p) + (st(mk.reshape(mem_shape)), st(mv.reshape(mem_shape))) \
        + tuple(st(a) for a in new_s)
```

```python
import functools

import jax
import jax.numpy as jnp
import numpy as np
from jax import lax
from jax.experimental import pallas as pl
from jax.experimental.pallas import tpu as pltpu

D_MODEL = 1024
CHUNK_SHIFT = 6
DSA_HEADS = 8
FOX_HEADS = 8
HEAD_DIM = 64
DSA_KV_HEADS = 2
IDX_HEADS = 4
IDX_DIM = 64
TOPK_MAX = 256
ROPE_THETA = 500000.0
ROT_DIM = HEAD_DIM // 4
N_MEM = 256
MEM_HEADS = 4
MEM_HEAD_DIM = D_MODEL // MEM_HEADS
PEER_HEADS = 8
PEER_KEY_DIM = 256
N_KEYS = 128
PEER_TOPK = 16
EPS = 1e-6
LANES = 128
VMEM_LIMIT = 48 * 1024 * 1024

F32 = jnp.float32
BF16 = jnp.bfloat16
I32 = jnp.int32

INT_MIN = int(np.iinfo(np.int32).min)
_NEG_INF_BITS = int(np.array(-np.inf, np.float32).view(np.int32))
KEY_NEG_INF = _NEG_INF_BITS ^ 0x7FFFFFFF
KEY_NEG_INF = KEY_NEG_INF - (1 << 32) if KEY_NEG_INF >= (1 << 31) else KEY_NEG_INF
MASK_VALUE = -1e30
MASK_BITS = int(np.array(MASK_VALUE, np.float32).view(np.int32))

SEG_Q = (0, 512)
SEG_K = (512, 640)
SEG_V = (640, 768)
SEG_QI = (768, 1024)
SEG_KI = (1024, 1152)
SEG_MISC = (1152, 1280)
SEG_FQ = (1280, 1792)
SEG_FK = (1792, 2304)
SEG_FV = (2304, 2816)
IN_COLS = 2816

CAND = [(a, b) for a in range(PEER_TOPK) for b in range(PEER_TOPK) if (a + 1) * (b + 1) <= PEER_TOPK]
N_CAND = len(CAND)


def _dot(a, b):
    return jnp.dot(a, b, preferred_element_type=F32)


def _dot_t(a, b):
    return lax.dot_general(a, b, (((1,), (1,)), ((), ())), preferred_element_type=F32)


def _rms(x, g):
    return x * lax.rsqrt(jnp.mean(x * x, axis=-1, keepdims=True) + EPS) * g


def _inproj_kernel(x_ref, g_ref, w_ref, b_ref, cos_ref, sa_ref, sb_ref,
                   aq_ref, ak_ref, av_ref, qi_ref, ki_ref, misc_ref, fq_ref, fk_ref, fv_ref,
                   akb_ref, avb_ref, kib_ref, fkb_ref, fvb_ref):
    xn = _rms(x_ref[...], g_ref[...]).astype(BF16)
    cos, sa, sb = cos_ref[...], sa_ref[...], sb_ref[...]

    def proj(seg):
        return _dot(xn, w_ref[:, seg[0]:seg[1]])

    def rope(y):
        outs = []
        for j in range(y.shape[1] // LANES):
            s = y[:, j * LANES:(j + 1) * LANES]
            outs.append(s * cos + pltpu.roll(s, LANES - ROT_DIM // 2, 1) * sa
                        + pltpu.roll(s, ROT_DIM // 2, 1) * sb)
        return outs[0] if len(outs) == 1 else jnp.concatenate(outs, axis=1)

    scale = HEAD_DIM ** -0.5
    aq_ref[...] = (rope(proj(SEG_Q)) * scale).astype(BF16)
    ak = rope(proj(SEG_K))
    ak_ref[...] = ak
    akb_ref[...] = ak.astype(BF16)
    av = proj(SEG_V)
    av_ref[...] = av
    avb_ref[...] = av.astype(BF16)
    qi_ref[...] = rope(proj(SEG_QI)).astype(BF16)
    ki = rope(proj(SEG_KI))
    ki_ref[...] = ki
    kib_ref[...] = ki.astype(BF16)
    z = proj(SEG_MISC)
    zb = z + b_ref[...]
    logf = jnp.minimum(zb, 0.0) - jnp.log1p(jnp.exp(-jnp.abs(zb)))
    lane = lax.broadcasted_iota(I32, z.shape, 1)
    misc_ref[...] = jnp.where((lane >= IDX_HEADS) & (lane < IDX_HEADS + FOX_HEADS), logf, z)
    fq_ref[...] = (proj(SEG_FQ) * scale).astype(BF16)
    fk = proj(SEG_FK)
    fk_ref[...] = fk
    fkb_ref[...] = fk.astype(BF16)
    fv = proj(SEG_FV)
    fv_ref[...] = fv
    fvb_ref[...] = fv.astype(BF16)


def _inproj(x2, g, w_cat, b_misc, cos, sa, sb, tb):
    n = x2.shape[0]
    ntab = cos.shape[0] // tb
    row = lambda w: pl.BlockSpec((tb, w), lambda i: (i, 0))
    tab = pl.BlockSpec((tb, LANES), lambda i: (i % ntab, 0))
    const = lambda s: pl.BlockSpec(s, lambda i: (0, 0))
    widths = [(512, BF16), (128, F32), (128, F32), (256, BF16), (128, F32), (128, F32),
              (512, BF16), (512, F32), (512, F32),
              (128, BF16), (128, BF16), (128, BF16), (512, BF16), (512, BF16)]
    return pl.pallas_call(
        _inproj_kernel,
        grid=(n // tb,),
        in_specs=[row(D_MODEL), const((1, D_MODEL)), const((D_MODEL, IN_COLS)), const((1, LANES)), tab, tab, tab],
        out_specs=[row(w) for w, _ in widths],
        out_shape=[jax.ShapeDtypeStruct((n, w), dt) for w, dt in widths],
        compiler_params=pltpu.CompilerParams(dimension_semantics=("arbitrary",), vmem_limit_bytes=VMEM_LIMIT),
        name="inproj",
    )(x2, g, w_cat, b_misc, cos, sa, sb)


def _split3(x):
    hi = x.astype(BF16)
    r1 = x - hi.astype(F32)
    mid = r1.astype(BF16)
    lo = (r1 - mid.astype(F32)).astype(BF16)
    return hi, mid, lo


def _cumsum_kernel(x_ref, tri_ref, o_ref):
    tri = tri_ref[...]

    def body(c, carry):
        hi, mid, lo = _split3(x_ref[0, c])
        out = _dot(hi, tri) + _dot(mid, tri) + _dot(lo, tri) + carry
        o_ref[0, c] = out
        return out[:, LANES - 1:LANES]

    lax.fori_loop(0, x_ref.shape[1], body, jnp.zeros((FOX_HEADS, 1), F32))


def _cumsum(logf_t, tri):
    b, nchunk = logf_t.shape[:2]
    spec = pl.BlockSpec((1, nchunk, FOX_HEADS, LANES), lambda i: (i, 0, 0, 0))
    return pl.pallas_call(
        _cumsum_kernel,
        grid=(b,),
        in_specs=[spec, pl.BlockSpec((LANES, LANES), lambda i: (0, 0))],
        out_specs=spec,
        out_shape=jax.ShapeDtypeStruct(logf_t.shape, F32),
        name="cumsum",
    )(logf_t, tri)


def _dsa_kernel(aq_ref, qi_ref, misc_ref, k_ref, v_ref, ki_ref, tri_ref, o_ref, s_ref,
                *, tq, kb, n_keys, q_off, topk):
    i = pl.program_id(1)
    sub = kb // LANES
    q_last = q_off + (i + 1) * tq - 1
    adm_len = jnp.minimum(n_keys, ((q_last >> CHUNK_SHIFT) + 1) << CHUNK_SHIFT)
    nkb = (adm_len + kb - 1) // kb
    qchunk = (q_off + i * tq + lax.broadcasted_iota(I32, (tq, 1), 0)) >> CHUNK_SHIFT
    lane = lax.broadcasted_iota(I32, (tq, LANES), 1)
    lo_half = lane < HEAD_DIM

    zero = jnp.zeros((), BF16)
    qh = []
    for j in range(IDX_HEADS // 2):
        slab = qi_ref[:, j * LANES:(j + 1) * LANES]
        qh += [jnp.where(lo_half, slab, zero), jnp.where(lo_half, zero, slab)]
    wi = [misc_ref[:, h:h + 1] for h in range(IDX_HEADS)]

    def p1(c, _):
        off = pl.multiple_of(c * kb, kb)
        kib = ki_ref[0, pl.ds(off, kb), :]
        sc = jnp.zeros((tq, kb), F32)
        for h in range(IDX_HEADS):
            sc = sc + jnp.maximum(_dot_t(qh[h], kib), 0.0) * wi[h]
        kpos = off + lax.broadcasted_iota(I32, (1, kb), 1)
        ok = ((kpos >> CHUNK_SHIFT) <= qchunk) & (kpos < n_keys)
        bits = lax.bitcast_convert_type(sc, I32)
        key = jnp.where(ok, bits ^ ((bits >> 31) & 0x7FFFFFFF), KEY_NEG_INF)
        for j in range(sub):
            s_ref[c * sub + j] = key[:, j * LANES:(j + 1) * LANES]
        return 0

    lax.fori_loop(0, nkb, p1, 0)

    def count_ge(trial):
        tb = jnp.broadcast_to(trial, (tq, LANES))

        def body(c, acc):
            for j in range(sub):
                acc = acc + (s_ref[c * sub + j] >= tb).astype(I32)
            return acc

        acc = lax.fori_loop(0, nkb, body, jnp.zeros((tq, LANES), I32))
        return jnp.sum(acc, axis=1, keepdims=True)

    def bisect(it, cand):
        trial = cand | (jnp.int32(1) << (31 - it))
        cnt = count_ge(trial ^ INT_MIN)
        return jnp.where(cnt >= topk, trial, cand)

    vstar = lax.fori_loop(0, 32, bisect, jnp.zeros((tq, 1), I32)) ^ INT_MIN
    need = (topk - count_ge(vstar + 1)).astype(F32)

    vb = jnp.broadcast_to(vstar, (tq, LANES))
    tri = tri_ref[...]

    def p2b(c, carry):
        for j in range(sub):
            blk = s_ref[c * sub + j]
            eq = blk == vb
            pref = _dot(jnp.where(eq, 1.0, 0.0).astype(BF16), tri)
            sel = ((blk > vb) | (eq & (pref + carry <= need))) & (blk > KEY_NEG_INF)
            s_ref[c * sub + j] = jnp.where(sel, 0, MASK_BITS)
            carry = carry + pref[:, LANES - 1:LANES]
        return carry

    lax.fori_loop(0, nkb, p2b, jnp.zeros((tq, 1), F32))

    rep = DSA_HEADS // DSA_KV_HEADS
    res = []
    for g in range(DSA_KV_HEADS):
        in_g = lo_half if g == 0 else jnp.logical_not(lo_half)
        q_all = jnp.concatenate(
            [jnp.where(in_g, aq_ref[:, r * LANES:(r + 1) * LANES], zero) for r in range(rep)], axis=0)

        in_g_k = lax.broadcasted_iota(I32, (kb, LANES), 1) < HEAD_DIM
        if g == 1:
            in_g_k = jnp.logical_not(in_g_k)

        def p3(c, carry):
            m, acc = carry
            off = pl.multiple_of(c * kb, kb)
            lg = _dot_t(q_all, k_ref[0, pl.ds(off, kb), :])
            bias = jnp.concatenate(
                [lax.bitcast_convert_type(s_ref[c * sub + j], F32) for j in range(sub)], axis=1)
            lg = lg + jnp.concatenate([bias] * rep, axis=0)
            mn = jnp.maximum(m, jnp.max(lg, axis=1, keepdims=True))
            a = jnp.exp(m - mn)
            p = jnp.exp(lg - mn)
            vone = jnp.where(in_g_k, v_ref[0, pl.ds(off, kb), :], jnp.ones((), BF16))
            acc = a * acc + _dot(p.astype(BF16), vone)
            return mn, acc

        init = (jnp.full((rep * tq, 1), MASK_VALUE, F32), jnp.zeros((rep * tq, LANES), F32))
        _, acc = lax.fori_loop(0, nkb, p3, init)
        denom = acc[:, HEAD_DIM * (1 - g):HEAD_DIM * (1 - g) + 1]
        res.append(acc / denom)
    for r in range(rep):
        o_ref[:, r * LANES:(r + 1) * LANES] = jnp.where(
            lo_half, res[0][r * tq:(r + 1) * tq], res[1][r * tq:(r + 1) * tq]).astype(BF16)


def _dsa(aq, qi, misc, k_b, v_b, ki_b, tri, *, batch, t, tq, kb, n_keys, q_off, topk):
    lp = k_b.shape[1]
    nq = t // tq
    qrow = lambda w: pl.BlockSpec((tq, w), lambda b, i: (b * nq + i, 0))
    keys = pl.BlockSpec((1, lp, LANES), lambda b, i: (b, 0, 0))
    kern = functools.partial(_dsa_kernel, tq=tq, kb=kb, n_keys=n_keys, q_off=q_off, topk=topk)
    return pl.pallas_call(
        kern,
        grid=(batch, nq),
        in_specs=[qrow(512), qrow(256), qrow(LANES), keys, keys, keys,
                  pl.BlockSpec((LANES, LANES), lambda b, i: (0, 0))],
        out_specs=qrow(512),
        out_shape=jax.ShapeDtypeStruct((batch * t, 512), BF16),
        scratch_shapes=[pltpu.VMEM((lp // LANES, tq, LANES), I32)],
        compiler_params=pltpu.CompilerParams(dimension_semantics=("arbitrary", "arbitrary"),
                                             vmem_limit_bytes=VMEM_LIMIT),
        name="dsa",
    )(aq, qi, misc, k_b, v_b, ki_b, tri)


def _fox_kernel(q_ref, k_ref, v_ref, dq_ref, dk_ref, o_ref, *, tq, kb, n_keys, q_off):
    i = pl.program_id(2)
    nkb = (jnp.minimum(n_keys, q_off + (i + 1) * tq) + kb - 1) // kb
    lane = lax.broadcasted_iota(I32, (tq, LANES), 1)
    lo_half = lane < HEAD_DIM
    zero = jnp.zeros((), BF16)
    q = q_ref[...]
    q_all = jnp.concatenate([jnp.where(lo_half, q, zero), jnp.where(lo_half, zero, q)], axis=0)
    dq = dq_ref[0, 0]
    qpos = q_off + i * tq + lax.broadcasted_iota(I32, (tq, 1), 0)

    def body(c, carry, masked):
        off = pl.multiple_of(c * kb, kb)
        lg = _dot_t(q_all, k_ref[0, pl.ds(off, kb), :])
        vblk = v_ref[0, pl.ds(off, kb), :]
        dk = dk_ref[0, 0, c]
        if masked:
            kpos = off + lax.broadcasted_iota(I32, (1, kb), 1)
            ok = (kpos <= qpos) & (kpos < n_keys)
        out = []
        for e in range(2):
            m, acc = carry[e]
            lge = lg[e * tq:(e + 1) * tq] + dq[:, e:e + 1] - dk[e:e + 1, :]
            if masked:
                lge = jnp.where(ok, lge, MASK_VALUE)
            mn = jnp.maximum(m, jnp.max(lge, axis=1, keepdims=True))
            a = jnp.exp(m - mn)
            p = jnp.exp(lge - mn)
            vone = jnp.where(lo_k if e == 0 else jnp.logical_not(lo_k), vblk, jnp.ones((), BF16))
            acc = a * acc + _dot(p.astype(BF16), vone)
            out.append((mn, acc))
        return tuple(out)

    lo_k = lax.broadcasted_iota(I32, (kb, LANES), 1) < HEAD_DIM
    one = (jnp.full((tq, 1), MASK_VALUE, F32), jnp.zeros((tq, LANES), F32))
    n_full = jnp.minimum((q_off + i * tq + 1) // kb, n_keys // kb)
    carry = lax.fori_loop(0, n_full, functools.partial(body, masked=False), (one, one))
    (_, a0), (_, a1) = lax.fori_loop(n_full, nkb, functools.partial(body, masked=True), carry)
    o_ref[...] = jnp.where(lo_half, a0 / a0[:, HEAD_DIM:HEAD_DIM + 1], a1 / a1[:, 0:1]).astype(BF16)


def _fox(fq, k_b, v_b, dq_p, dk_p, *, batch, t, tq, kb, n_keys, q_off):
    lp = k_b.shape[1]
    nq = t // tq
    npair = FOX_HEADS // 2
    kern = functools.partial(_fox_kernel, tq=tq, kb=kb, n_keys=n_keys, q_off=q_off)
    kv = pl.BlockSpec((1, lp, LANES), lambda b, p, i: (b, 0, p))
    qo = pl.BlockSpec((tq, LANES), lambda b, p, i: (b * nq + i, p))
    return pl.pallas_call(
        kern,
        grid=(batch, npair, nq),
        in_specs=[qo, kv, kv,
                  pl.BlockSpec((1, 1, tq, 2), lambda b, p, i: (b, p, i, 0)),
                  pl.BlockSpec((1, 1, lp // kb, 2, kb), lambda b, p, i: (b, p, 0, 0, 0))],
        out_specs=qo,
        out_shape=jax.ShapeDtypeStruct((batch * t, FOX_HEADS * HEAD_DIM), BF16),
        compiler_params=pltpu.CompilerParams(dimension_semantics=("arbitrary",) * 3,
                                             vmem_limit_bytes=VMEM_LIMIT),
        name="fox",
    )(fq, k_b, v_b, dq_p, dk_p)


def _memkv_kernel(x_ref, g_ref, w_ref, k_ref, v_ref, kb_ref, vb_ref):
    xn = _rms(x_ref[...], g_ref[...]).astype(BF16)
    k = _dot(xn, w_ref[:, :D_MODEL])
    v = _dot(xn, w_ref[:, D_MODEL:])
    k_ref[...] = k
    v_ref[...] = v
    kb_ref[...] = k.astype(BF16)
    vb_ref[...] = v.astype(BF16)


def _memkv(mem2, g, w_kv):
    n = mem2.shape[0]
    row = pl.BlockSpec((N_MEM, D_MODEL), lambda i: (i, 0))
    return pl.pallas_call(
        _memkv_kernel,
        grid=(n // N_MEM,),
        in_specs=[row, pl.BlockSpec((1, D_MODEL), lambda i: (0, 0)),
                  pl.BlockSpec((D_MODEL, 2 * D_MODEL), lambda i: (0, 0))],
        out_specs=[row] * 4,
        out_shape=[jax.ShapeDtypeStruct((n, D_MODEL), dt) for dt in (F32, F32, BF16, BF16)],
        compiler_params=pltpu.CompilerParams(dimension_semantics=("arbitrary",), vmem_limit_bytes=VMEM_LIMIT),
        name="memkv",
    )(mem2, g, w_kv)


def _post_kernel(x_ref, oa_ref, of_ref, woa_ref, wof_ref, gm_ref, wq_ref, mk_ref, mv_ref, wo_ref, gf_ref,
                 h_ref, xn_ref):
    h = x_ref[...] + _dot(oa_ref[...], woa_ref[...]) + _dot(of_ref[...], wof_ref[...])
    hn = _rms(h, gm_ref[...]).astype(BF16)
    q = (_dot(hn, wq_ref[...]) * (MEM_HEAD_DIM ** -0.5)).astype(BF16)
    outs = []
    for hd in range(MEM_HEADS):
        cs = slice(hd * MEM_HEAD_DIM, (hd + 1) * MEM_HEAD_DIM)
        lg = _dot_t(q[:, cs], mk_ref[0, :, cs])
        p = jnp.exp(lg - jnp.max(lg, axis=1, keepdims=True))
        p = p / jnp.sum(p, axis=1, keepdims=True)
        outs.append(_dot(p.astype(BF16), mv_ref[0, :, cs]))
    o = jnp.concatenate(outs, axis=1).astype(BF16)
    h = h + _dot(o, wo_ref[...])
    h_ref[...] = h
    xn_ref[...] = _rms(h, gf_ref[...])


def _post(x2, oa, of, woa, wof, gm, wq, mk_b, mv_b, wo, gf, *, batch, t, tb):
    nt = t // tb
    row = lambda w: pl.BlockSpec((tb, w), lambda b, i: (b * nt + i, 0))
    const = lambda s: pl.BlockSpec(s, lambda b, i: (0, 0))
    mem = pl.BlockSpec((1, N_MEM, D_MODEL), lambda b, i: (b, 0, 0))
    return pl.pallas_call(
        _post_kernel,
        grid=(batch, nt),
        in_specs=[row(D_MODEL), row(512), row(512), const((512, D_MODEL)), const((512, D_MODEL)),
                  const((1, D_MODEL)), const((D_MODEL, D_MODEL)), mem, mem, const((D_MODEL, D_MODEL)),
                  const((1, D_MODEL))],
        out_specs=[row(D_MODEL), row(D_MODEL)],
        out_shape=[jax.ShapeDtypeStruct((batch * t, D_MODEL), F32)] * 2,
        compiler_params=pltpu.CompilerParams(dimension_semantics=("arbitrary", "arbitrary"),
                                             vmem_limit_bytes=VMEM_LIMIT),
        name="post",
    )(x2, oa, of, woa, wof, gm, wq, mk_b, mv_b, wo, gf)


def _top16(s):
    r = s.shape[0]
    row = lax.broadcasted_iota(I32, s.shape, 0).astype(F32)
    vals, ids = [], []
    for _ in range(PEER_TOPK):
        m = jnp.max(s, axis=0, keepdims=True)
        am = jnp.min(jnp.where(s == m, row, float(r)), axis=0, keepdims=True)
        vals.append(m)
        ids.append(am)
        s = jnp.where(row == am, -jnp.inf, s)
    return jnp.concatenate(vals, axis=0), jnp.concatenate(ids, axis=0)


def _peer_sel_kernel(x_ref, wqt_ref, k1_ref, k2_ref, eidx_ref, gate_ref, qt_scr, e_scr, g_scr):
    half = PEER_KEY_DIM // 2
    qt_scr[...] = _dot_t(wqt_ref[...], x_ref[...].astype(BF16)).astype(BF16)
    n = qt_scr.shape[1]
    pad = (-N_CAND) % 8

    def head(h, _):
        r0 = pl.multiple_of(h * PEER_KEY_DIM, PEER_KEY_DIM)
        v1, i1 = _top16(_dot(k1_ref[h], qt_scr[pl.ds(r0, half), :]))
        v2, i2 = _top16(_dot(k2_ref[h], qt_scr[pl.ds(r0 + half, half), :]))
        cand = jnp.concatenate([v1[a:a + 1] + v2[b:b + 1] for a, b in CAND]
                               + [jnp.full((pad, n), -jnp.inf, F32)], axis=0)
        cidx = jnp.concatenate([i1[a:a + 1] * float(N_KEYS) + i2[b:b + 1] for a, b in CAND]
                               + [jnp.zeros((pad, n), F32)], axis=0)
        sv, sr = _top16(cand)
        row = lax.broadcasted_iota(I32, cand.shape, 0).astype(F32)
        eid = jnp.concatenate(
            [jnp.sum(jnp.where(row == sr[k:k + 1], cidx, 0.0), axis=0, keepdims=True) for k in range(PEER_TOPK)],
            axis=0)
        p = jnp.exp(sv - sv[0:1])
        o0 = pl.multiple_of(h * PEER_TOPK, PEER_TOPK)
        e_scr[pl.ds(o0, PEER_TOPK), :] = eid
        g_scr[pl.ds(o0, PEER_TOPK), :] = p / jnp.sum(p, axis=0, keepdims=True)
        return 0

    lax.fori_loop(0, PEER_HEADS, head, 0)
    eidx_ref[...] = e_scr[...].T.astype(I32)
    gate_ref[...] = g_scr[...].T


def _peer_sel(xn2, wq_t, keys1, keys2, tb):
    n = xn2.shape[0]
    kspec = pl.BlockSpec((PEER_HEADS, N_KEYS, PEER_KEY_DIM // 2), lambda i: (0, 0, 0))
    out = pl.BlockSpec((tb, PEER_HEADS * PEER_TOPK), lambda i: (i, 0))
    return pl.pallas_call(
        _peer_sel_kernel,
        grid=(n // tb,),
        in_specs=[pl.BlockSpec((tb, D_MODEL), lambda i: (i, 0)),
                  pl.BlockSpec((PEER_HEADS * PEER_KEY_DIM, D_MODEL), lambda i: (0, 0)), kspec, kspec],
        out_specs=[out, out],
        out_shape=[jax.ShapeDtypeStruct((n, PEER_HEADS * PEER_TOPK), I32),
                   jax.ShapeDtypeStruct((n, PEER_HEADS * PEER_TOPK), F32)],
        scratch_shapes=[pltpu.VMEM((PEER_HEADS * PEER_KEY_DIM, tb), BF16)]
        + [pltpu.VMEM((PEER_HEADS * PEER_TOPK, tb), F32)] * 2,
        compiler_params=pltpu.CompilerParams(dimension_semantics=("arbitrary",), vmem_limit_bytes=VMEM_LIMIT),
        name="peer_sel",
    )(xn2, wq_t, keys1, keys2)


N_PICK = PEER_HEADS * PEER_TOPK
P_AHEAD = 6
N_SLOT = P_AHEAD + 4
GROUP = 8
HI_MASK = -65536


def _gelu(x):
    return 0.5 * x * (1.0 + lax.erf(x * (2.0 ** -0.5)))


def _peer_ffn_kernel(eidx_ref, gate_ref, x_ref, h_ref, fg_ref, ones_ref, eye_ref, uv_ref, y_ref,
                     buf, hd_scr, w_scr, wc_scr, acc, sem, *, tt):
    @pl.when(pl.program_id(0) == 0)
    def _():
        buf[...] = jnp.zeros(buf.shape, I32)
        hd_scr[...] = jnp.zeros(hd_scr.shape, F32)
        w_scr[...] = jnp.zeros(w_scr.shape, F32)
        wc_scr[...] = jnp.zeros(wc_scr.shape, F32)

    n_tiles = D_MODEL // LANES
    n_groups = N_PICK // GROUP
    c1_at, c2_at = 1, 10

    def expert_copy(t, j, slot):
        return pltpu.make_async_copy(uv_ref.at[eidx_ref[t * N_PICK + j]], buf.at[slot, :, j, :], sem.at[slot])

    def packed_rows(slot, rows):
        return jnp.concatenate([buf[slot, s, rows, :] for s in range(n_tiles)], axis=1)

    def lane_tiles_sum(a):
        out = a[:, 0:LANES]
        for c in range(1, n_tiles):
            out = out + a[:, c * LANES:(c + 1) * LANES]
        return out

    def step(t, r, issue, gather_u, mix1, mix2):
        slot_v = (r - 3) % N_SLOT
        par = r % 2
        if gather_u:
            for _ in range(N_PICK):
                pltpu.make_async_copy(uv_ref.at[0], buf.at[r, :, 0, :], sem.at[r]).wait()
            xrow = jnp.broadcast_to(x_ref[pl.ds(t, 1), :], (GROUP, D_MODEL))
        acc8 = jnp.zeros((GROUP, D_MODEL), F32)
        part = w_new = wcol_new = None
        for k in range(n_groups):
            rows = slice(k * GROUP, (k + 1) * GROUP)
            if issue:
                for e in range(GROUP):
                    expert_copy(t + P_AHEAD, k * GROUP + e, (r + P_AHEAD) % N_SLOT).start(priority=e % 2)
            if gather_u:
                if part is not None:
                    hd_scr[par, (k - 1) * GROUP:k * GROUP, :] = part
                u = lax.bitcast_convert_type(packed_rows(r, rows) & HI_MASK, F32)
                part = lane_tiles_sum(u * xrow)
            v = lax.bitcast_convert_type(packed_rows(slot_v, rows) << 16, F32)
            acc8 = acc8 + v * jnp.concatenate([wc_scr[rows, :]] * n_tiles, axis=1)
            if mix1 and k == c1_at:
                hdp = hd_scr[1 - par]
                hi = hdp.astype(BF16)
                lo = (hdp - hi.astype(F32)).astype(BF16)
                ones8 = ones_ref[0:GROUP, :]
                hd = _dot_t(ones8, hi) + _dot_t(ones8, lo)
                w_new = gate_ref[pl.ds(jnp.maximum(t - 1, 0), 1), :] * _gelu(hd)
            if mix2 and k == c2_at:
                diag = (eye_ref[...] * jnp.broadcast_to(w_scr[0:1, :], (LANES, LANES))).astype(BF16)
                wcol_new = _dot(diag, ones_ref[...])
        if gather_u:
            hd_scr[par, (n_groups - 1) * GROUP:, :] = part
        if mix2:
            wc_scr[...] = wcol_new
        if mix1:
            w_scr[...] = w_new
        acc[pl.ds(jnp.maximum(t - 3, 0), 1), :] = jnp.sum(acc8, axis=0, keepdims=True)

    for t0 in range(P_AHEAD):
        for j in range(N_PICK):
            expert_copy(t0, j, t0).start(priority=j % 2)

    n_round = (tt - P_AHEAD) // N_SLOT

    def one_round(q, _):
        for r in range(N_SLOT):
            step(q * N_SLOT + r, r, True, True, True, True)
        return 0

    lax.fori_loop(0, n_round, one_round, 0)
    for t in range(n_round * N_SLOT, tt + 3):
        step(t, t % N_SLOT, t + P_AHEAD < tt, t < tt, t <= tt, t <= tt + 1)
    y_ref[...] = _rms(h_ref[...] + acc[...], fg_ref[...])


def _peer_ffn(eidx, gate, xn2, h2, fg, uv, tt):
    n = xn2.shape[0]
    row = lambda w: pl.BlockSpec((tt, w), lambda i: (i, 0))
    const = lambda s: pl.BlockSpec(s, lambda i: (0, 0))
    return pl.pallas_call(
        functools.partial(_peer_ffn_kernel, tt=tt),
        grid=(n // tt,),
        in_specs=[pl.BlockSpec((tt * N_PICK,), lambda i: (i,), memory_space=pltpu.SMEM),
                  row(N_PICK), row(D_MODEL), row(D_MODEL), const((1, D_MODEL)), const((LANES, LANES)),
                  const((LANES, LANES)), pl.BlockSpec(memory_space=pl.ANY)],
        out_specs=row(D_MODEL),
        out_shape=jax.ShapeDtypeStruct((n, D_MODEL), F32),
        scratch_shapes=[pltpu.VMEM((N_SLOT, D_MODEL // LANES, N_PICK, LANES), I32), pltpu.VMEM((2, N_PICK, LANES), F32),
                        pltpu.VMEM((GROUP, LANES), F32), pltpu.VMEM((N_PICK, LANES), F32),
                        pltpu.VMEM((tt, D_MODEL), F32), pltpu.SemaphoreType.DMA((N_SLOT,))],
        compiler_params=pltpu.CompilerParams(dimension_semantics=("arbitrary",), vmem_limit_bytes=VMEM_LIMIT),
        name="peer_ffn",
    )(eidx.reshape(-1), gate, xn2, h2, fg, jnp.ones((LANES, LANES), BF16), jnp.eye(LANES, dtype=F32), uv)


def _rope_tables(pos):
    inv_freq = ROPE_THETA ** (-jnp.arange(0, ROT_DIM, 2, dtype=F32) / ROT_DIM)
    ang = pos.astype(F32)[:, None] * inv_freq[None, :]
    cos, sin = jnp.cos(ang), jnp.sin(ang)
    n = pos.shape[0]
    rest = HEAD_DIM - ROT_DIM
    hr = ROT_DIM // 2
    c = jnp.concatenate([cos, cos, jnp.ones((n, rest), F32)], axis=1)
    a = jnp.concatenate([-sin, jnp.zeros((n, hr + rest), F32)], axis=1)
    b = jnp.concatenate([jnp.zeros((n, hr), F32), sin, jnp.zeros((n, rest), F32)], axis=1)
    return tuple(jnp.tile(z, (1, LANES // HEAD_DIM)) for z in (c, a, b))


Q_PERM = np.concatenate([np.r_[j * HEAD_DIM:(j + 1) * HEAD_DIM, (4 + j) * HEAD_DIM:(5 + j) * HEAD_DIM]
                         for j in range(4)])


def _pack_uv(u, v):
    bits = lambda a: lax.bitcast_convert_type(a.astype(jnp.bfloat16), jnp.uint16).astype(jnp.uint32)
    return lax.bitcast_convert_type((bits(u) << 16) | bits(v), I32).reshape(-1, D_MODEL // LANES, LANES)


def _prep_weights(norm_mix_g, w_in, b_fox_f, w_out, norm_mem_g, w_mem_q, w_mem_o, norm_ffn_g, peer_w_q,
                  peer_keys1, peer_keys2, peer_u, peer_v):
    sp = np.cumsum([0, 512, 128, 128, 256, 64, 4, 512, 512, 512, 8])
    a_q, a_k, a_v, a_qi, a_ki, a_w, f_q, f_k, f_v, f_f = [w_in[:, sp[i]:sp[i + 1]] for i in range(10)]
    zpad = jnp.zeros((D_MODEL, LANES - IDX_HEADS - FOX_HEADS), F32)
    w_cat = jnp.concatenate([a_q[:, Q_PERM], a_k, a_v, a_qi, a_ki, a_ki, a_w, f_f, zpad, f_q, f_k, f_v],
                            axis=1).astype(BF16)
    b_misc = jnp.concatenate([jnp.zeros((IDX_HEADS,), F32), b_fox_f.astype(F32),
                              jnp.zeros((LANES - IDX_HEADS - FOX_HEADS,), F32)])[None, :]
    r = lambda g: g.astype(F32)[None, :]
    return dict(
        g_mix=r(norm_mix_g), w_cat=w_cat, b_misc=b_misc,
        woa=w_out[:512][Q_PERM].astype(BF16), wof=w_out[512:].astype(BF16),
        g_mem=r(norm_mem_g), wq=w_mem_q.astype(BF16), wo=w_mem_o.astype(BF16), g_ffn=r(norm_ffn_g),
        wq_t=peer_w_q.T.astype(BF16), keys1=peer_keys1.astype(BF16), keys2=peer_keys2.astype(BF16),
        uv=_pack_uv(peer_u, peer_v),
    )


def _round_up(x, m):
    return (x + m - 1) // m * m


def _layer(x, q_off, past, mk_b, mv_b, w, fg, tri, cfg):
    batch, t, _ = x.shape
    n = batch * t
    x2 = x.reshape(n, D_MODEL)
    tabs = _rope_tables(q_off + jnp.arange(t, dtype=I32))
    tb_in = cfg["tb_in"]
    if tb_in > t:
        tabs = tuple(jnp.tile(z, (tb_in // t, 1)) for z in tabs)
    (aq, ak, av, qi, ki, misc, fq, fk, fv, akb, avb, kib, fkb, fvb) = _inproj(
        x2, w["g_mix"], w["w_cat"], w["b_misc"], *tabs, tb_in)
    logf = misc[:, IDX_HEADS:IDX_HEADS + FOX_HEADS].reshape(batch, t, FOX_HEADS)
    new_rows = (ak.reshape(batch, t, DSA_KV_HEADS, HEAD_DIM), av.reshape(batch, t, DSA_KV_HEADS, HEAD_DIM),
                ki[:, :IDX_DIM].reshape(batch, t, IDX_DIM), fk.reshape(batch, t, FOX_HEADS, HEAD_DIM),
                fv.reshape(batch, t, FOX_HEADS, HEAD_DIM), logf)

    kb = cfg["kb"]
    n_past = 0 if past is None else past[0].shape[1]
    n_keys = n_past + t
    lp = _round_up(n_keys, kb)

    def keys(new_b, old, width):
        new_b = new_b.reshape(batch, t, width)
        parts = [new_b] if old is None else [old.reshape(batch, n_past, -1).astype(BF16), new_b]
        if lp > n_keys:
            parts.append(jnp.zeros((batch, lp - n_keys, width), BF16))
        return parts[0] if len(parts) == 1 else jnp.concatenate(parts, axis=1)

    if past is None:
        k_all, v_all, ki_all = keys(akb, None, LANES), keys(avb, None, LANES), keys(kib, None, LANES)
        fk_all, fv_all = keys(fkb, None, 512), keys(fvb, None, 512)
        logf_all = logf
    else:
        p_k, p_v, p_ki, p_fk, p_fv, p_logf = past
        ki_dup = jnp.concatenate([p_ki, p_ki], axis=-1)
        k_all, v_all, ki_all = keys(akb, p_k, LANES), keys(avb, p_v, LANES), keys(kib, ki_dup, LANES)
        fk_all, fv_all = keys(fkb, p_fk, 512), keys(fvb, p_fv, 512)
        logf_all = jnp.concatenate([p_logf.astype(F32), logf], axis=1)

    lf = jnp.pad(logf_all, ((0, 0), (0, lp - n_keys), (0, 0)))
    lf_t = lf.reshape(batch, lp // LANES, LANES, FOX_HEADS).transpose(0, 1, 3, 2)
    d_t = _cumsum(lf_t, tri).transpose(0, 2, 1, 3).reshape(batch, FOX_HEADS, lp)
    npair = FOX_HEADS // 2
    dq_p = d_t[:, :, n_keys - t:n_keys].reshape(batch, npair, 2, t).transpose(0, 1, 3, 2)
    dk_p = d_t.reshape(batch, npair, 2, lp // kb, kb).transpose(0, 1, 3, 2, 4)

    topk = min(TOPK_MAX, n_keys // 4)
    oa = _dsa(aq, qi, misc, k_all, v_all, ki_all, tri, batch=batch, t=t, tq=cfg["tq_dsa"], kb=kb,
              n_keys=n_keys, q_off=q_off, topk=topk)
    of = _fox(fq, fk_all, fv_all, dq_p, dk_p, batch=batch, t=t, tq=cfg["tq_fox"], kb=kb,
              n_keys=n_keys, q_off=q_off)
    h2, xn2 = _post(x2, oa, of, w["woa"], w["wof"], w["g_mem"], w["wq"], mk_b, mv_b, w["wo"], w["g_ffn"],
                    batch=batch, t=t, tb=cfg["tb_post"])
    eidx, gate = _peer_sel(xn2, w["wq_t"], w["keys1"], w["keys2"], cfg["tb_sel"])
    y = _peer_ffn(eidx, gate, xn2, h2, fg, w["uv"], cfg["tt_ffn"])
    return y.reshape(batch, t, D_MODEL), new_rows


def _config(t, n_tokens, has_past):
    if has_past:
        return dict(tb_in=n_tokens, kb=384, tq_dsa=t, tq_fox=t, tb_post=t, tb_sel=n_tokens, tt_ffn=min(128, n_tokens))
    return dict(tb_in=min(512, t), kb=min(512, t), tq_dsa=min(128, t), tq_fox=min(1024, t), tb_post=min(256, t),
                tb_sel=min(256, n_tokens), tt_ffn=min(256, n_tokens))


def kernel(x_prompt, x_sample, cache_dsa_k, cache_dsa_v, cache_dsa_kidx, cache_fox_k, cache_fox_v, cache_fox_logf, cache_mem_k, cache_mem_v, mem_prompt, norm_mix_g, w_in, b_fox_f, w_out, mem_norm_g, w_mem_kv, norm_mem_g, w_mem_q, w_mem_o, norm_ffn_g, peer_w_q, peer_keys1, peer_keys2, peer_u, peer_v, final_norm_g):
    depth = w_in.shape[0]
    assert depth == 1, "single-layer trunk"
    l = 0
    bp, tp, _ = x_prompt.shape
    bs, ts, _ = x_sample.shape
    past_len = cache_dsa_k.shape[2]
    w = _prep_weights(norm_mix_g[l], w_in[l], b_fox_f[l], w_out[l], norm_mem_g[l], w_mem_q[l], w_mem_o[l],
                      norm_ffn_g[l], peer_w_q[l], peer_keys1[l], peer_keys2[l], peer_u[l], peer_v[l])
    fg = final_norm_g.astype(F32)[None, :]
    ii = np.arange(LANES)
    tri = jnp.asarray(ii[:, None] <= ii[None, :], BF16)

    mk, mv, mkb, mvb = _memkv(mem_prompt.reshape(bp * N_MEM, D_MODEL), mem_norm_g[l].astype(F32)[None, :],
                              w_mem_kv[l].astype(BF16))
    y_p, new_p = _layer(x_prompt, 0, None, mkb.reshape(bp, N_MEM, D_MODEL), mvb.reshape(bp, N_MEM, D_MODEL),
                        w, fg, tri, _config(tp, bp * tp, False))
    past = (cache_dsa_k[l], cache_dsa_v[l], cache_dsa_kidx[l], cache_fox_k[l], cache_fox_v[l], cache_fox_logf[l])
    y_s, new_s = _layer(x_sample, past_len, past, cache_mem_k[l].reshape(bs, N_MEM, D_MODEL).astype(BF16),
                        cache_mem_v[l].reshape(bs, N_MEM, D_MODEL).astype(BF16), w, fg, tri,
                        _config(ts, bs * ts, True))
    st = lambda a: a[None]
    mem_shape = (bp, N_MEM, MEM_HEADS, MEM_HEAD_DIM)
    return (y_p, y_s) + tuple(st(a) for a in new_p) + (st(mk.reshape(mem_shape)), st(mv.reshape(mem_shape))) \
        + tuple(st(a) for a in new_s)
```

```python
import functools

import jax
import jax.numpy as jnp
import numpy as np
from jax import lax
from jax.experimental import pallas as pl
from jax.experimental.pallas import tpu as pltpu

D_MODEL = 1024
CHUNK_SHIFT = 6
DSA_HEADS = 8
FOX_HEADS = 8
HEAD_DIM = 64
DSA_KV_HEADS = 2
IDX_HEADS = 4
IDX_DIM = 64
TOPK_MAX = 256
ROPE_THETA = 500000.0
ROT_DIM = HEAD_DIM // 4
N_MEM = 256
MEM_HEADS = 4
MEM_HEAD_DIM = D_MODEL // MEM_HEADS
PEER_HEADS = 8
PEER_KEY_DIM = 256
N_KEYS = 128
PEER_TOPK = 16
EPS = 1e-6
LANES = 128
VMEM_LIMIT = 48 * 1024 * 1024

F32 = jnp.float32
BF16 = jnp.bfloat16
I32 = jnp.int32

INT_MIN = int(np.iinfo(np.int32).min)
_NEG_INF_BITS = int(np.array(-np.inf, np.float32).view(np.int32))
KEY_NEG_INF = _NEG_INF_BITS ^ 0x7FFFFFFF
KEY_NEG_INF = KEY_NEG_INF - (1 << 32) if KEY_NEG_INF >= (1 << 31) else KEY_NEG_INF
MASK_VALUE = -1e30
MASK_BITS = int(np.array(MASK_VALUE, np.float32).view(np.int32))

SEG_Q = (0, 512)
SEG_K = (512, 640)
SEG_V = (640, 768)
SEG_QI = (768, 1024)
SEG_KI = (1024, 1152)
SEG_MISC = (1152, 1280)
SEG_FQ = (1280, 1792)
SEG_FK = (1792, 2304)
SEG_FV = (2304, 2816)
IN_COLS = 2816

CAND = [(a, b) for a in range(PEER_TOPK) for b in range(PEER_TOPK) if (a + 1) * (b + 1) <= PEER_TOPK]
N_CAND = len(CAND)


def _dot(a, b):
    return jnp.dot(a, b, preferred_element_type=F32)


def _dot_t(a, b):
    return lax.dot_general(a, b, (((1,), (1,)), ((), ())), preferred_element_type=F32)


def _rms(x, g):
    return x * lax.rsqrt(jnp.mean(x * x, axis=-1, keepdims=True) + EPS) * g


def _inproj_kernel(x_ref, g_ref, w_ref, b_ref, cos_ref, sa_ref, sb_ref,
                   aq_ref, ak_ref, av_ref, qi_ref, ki_ref, misc_ref, fq_ref, fk_ref, fv_ref,
                   akb_ref, avb_ref, kib_ref, fkb_ref, fvb_ref):
    xn = _rms(x_ref[...], g_ref[...]).astype(BF16)
    cos, sa, sb = cos_ref[...], sa_ref[...], sb_ref[...]

    def proj(seg):
        return _dot(xn, w_ref[:, seg[0]:seg[1]])

    def rope(y):
        outs = []
        for j in range(y.shape[1] // LANES):
            s = y[:, j * LANES:(j + 1) * LANES]
            outs.append(s * cos + pltpu.roll(s, LANES - ROT_DIM // 2, 1) * sa
                        + pltpu.roll(s, ROT_DIM // 2, 1) * sb)
        return outs[0] if len(outs) == 1 else jnp.concatenate(outs, axis=1)

    scale = HEAD_DIM ** -0.5
    aq_ref[...] = (rope(proj(SEG_Q)) * scale).astype(BF16)
    ak = rope(proj(SEG_K))
    ak_ref[...] = ak
    akb_ref[...] = ak.astype(BF16)
    av = proj(SEG_V)
    av_ref[...] = av
    avb_ref[...] = av.astype(BF16)
    qi_ref[...] = rope(proj(SEG_QI)).astype(BF16)
    ki = rope(proj(SEG_KI))
    ki_ref[...] = ki
    kib_ref[...] = ki.astype(BF16)
    z = proj(SEG_MISC)
    zb = z + b_ref[...]
    logf = jnp.minimum(zb, 0.0) - jnp.log1p(jnp.exp(-jnp.abs(zb)))
    lane = lax.broadcasted_iota(I32, z.shape, 1)
    misc_ref[...] = jnp.where((lane >= IDX_HEADS) & (lane < IDX_HEADS + FOX_HEADS), logf, z)
    fq_ref[...] = (proj(SEG_FQ) * scale).astype(BF16)
    fk = proj(SEG_FK)
    fk_ref[...] = fk
    fkb_ref[...] = fk.astype(BF16)
    fv = proj(SEG_FV)
    fv_ref[...] = fv
    fvb_ref[...] = fv.astype(BF16)


def _inproj(x2, g, w_cat, b_misc, cos, sa, sb, tb):
    n = x2.shape[0]
    ntab = cos.shape[0] // tb
    row = lambda w: pl.BlockSpec((tb, w), lambda i: (i, 0))
    tab = pl.BlockSpec((tb, LANES), lambda i: (i % ntab, 0))
    const = lambda s: pl.BlockSpec(s, lambda i: (0, 0))
    widths = [(512, BF16), (128, F32), (128, F32), (256, BF16), (128, F32), (128, F32),
              (512, BF16), (512, F32), (512, F32),
              (128, BF16), (128, BF16), (128, BF16), (512, BF16), (512, BF16)]
    return pl.pallas_call(
        _inproj_kernel,
        grid=(n // tb,),
        in_specs=[row(D_MODEL), const((1, D_MODEL)), const((D_MODEL, IN_COLS)), const((1, LANES)), tab, tab, tab],
        out_specs=[row(w) for w, _ in widths],
        out_shape=[jax.ShapeDtypeStruct((n, w), dt) for w, dt in widths],
        compiler_params=pltpu.CompilerParams(dimension_semantics=("arbitrary",), vmem_limit_bytes=VMEM_LIMIT),
        name="inproj",
    )(x2, g, w_cat, b_misc, cos, sa, sb)


def _split3(x):
    hi = x.astype(BF16)
    r1 = x - hi.astype(F32)
    mid = r1.astype(BF16)
    lo = (r1 - mid.astype(F32)).astype(BF16)
    return hi, mid, lo


def _cumsum_kernel(x_ref, tri_ref, o_ref):
    tri = tri_ref[...]

    def body(c, carry):
        hi, mid, lo = _split3(x_ref[0, c])
        out = _dot(hi, tri) + _dot(mid, tri) + _dot(lo, tri) + carry
        o_ref[0, c] = out
        return out[:, LANES - 1:LANES]

    lax.fori_loop(0, x_ref.shape[1], body, jnp.zeros((FOX_HEADS, 1), F32))


def _cumsum(logf_t, tri):
    b, nchunk = logf_t.shape[:2]
    spec = pl.BlockSpec((1, nchunk, FOX_HEADS, LANES), lambda i: (i, 0, 0, 0))
    return pl.pallas_call(
        _cumsum_kernel,
        grid=(b,),
        in_specs=[spec, pl.BlockSpec((LANES, LANES), lambda i: (0, 0))],
        out_specs=spec,
        out_shape=jax.ShapeDtypeStruct(logf_t.shape, F32),
        name="cumsum",
    )(logf_t, tri)


def _dsa_kernel(aq_ref, qi_ref, misc_ref, k_ref, v_ref, ki_ref, tri_ref, o_ref, s_ref,
                *, tq, kb, n_keys, q_off, topk):
    i = pl.program_id(1)
    sub = kb // LANES
    q_last = q_off + (i + 1) * tq - 1
    adm_len = jnp.minimum(n_keys, ((q_last >> CHUNK_SHIFT) + 1) << CHUNK_SHIFT)
    nkb = (adm_len + kb - 1) // kb
    qchunk = (q_off + i * tq + lax.broadcasted_iota(I32, (tq, 1), 0)) >> CHUNK_SHIFT
    lane = lax.broadcasted_iota(I32, (tq, LANES), 1)
    lo_half = lane < HEAD_DIM

    zero = jnp.zeros((), BF16)
    qh = []
    for j in range(IDX_HEADS // 2):
        slab = qi_ref[:, j * LANES:(j + 1) * LANES]
        qh += [jnp.where(lo_half, slab, zero), jnp.where(lo_half, zero, slab)]
    wi = [misc_ref[:, h:h + 1] for h in range(IDX_HEADS)]

    def p1(c, _):
        off = pl.multiple_of(c * kb, kb)
        kib = ki_ref[0, pl.ds(off, kb), :]
        sc = jnp.zeros((tq, kb), F32)
        for h in range(IDX_HEADS):
            sc = sc + jnp.maximum(_dot_t(qh[h], kib), 0.0) * wi[h]
        kpos = off + lax.broadcasted_iota(I32, (1, kb), 1)
        ok = ((kpos >> CHUNK_SHIFT) <= qchunk) & (kpos < n_keys)
        bits = lax.bitcast_convert_type(sc, I32)
        key = jnp.where(ok, bits ^ ((bits >> 31) & 0x7FFFFFFF), KEY_NEG_INF)
        for j in range(sub):
            s_ref[c * sub + j] = key[:, j * LANES:(j + 1) * LANES]
        return 0

    lax.fori_loop(0, nkb, p1, 0)

    def count_ge(trial):
        tb = jnp.broadcast_to(trial, (tq, LANES))

        def body(c, acc):
            for j in range(sub):
                acc = acc + (s_ref[c * sub + j] >= tb).astype(I32)
            return acc

        acc = lax.fori_loop(0, nkb, body, jnp.zeros((tq, LANES), I32))
        return jnp.sum(acc, axis=1, keepdims=True)

    def bisect(it, cand):
        trial = cand | (jnp.int32(1) << (31 - it))
        cnt = count_ge(trial ^ INT_MIN)
        return jnp.where(cnt >= topk, trial, cand)

    vstar = lax.fori_loop(0, 32, bisect, jnp.zeros((tq, 1), I32)) ^ INT_MIN
    need = (topk - count_ge(vstar + 1)).astype(F32)

    vb = jnp.broadcast_to(vstar, (tq, LANES))
    tri = tri_ref[...]

    def p2b(c, carry):
        for j in range(sub):
            blk = s_ref[c * sub + j]
            eq = blk == vb
            pref = _dot(jnp.where(eq, 1.0, 0.0).astype(BF16), tri)
            sel = ((blk > vb) | (eq & (pref + carry <= need))) & (blk > KEY_NEG_INF)
            s_ref[c * sub + j] = jnp.where(sel, 0, MASK_BITS)
            carry = carry + pref[:, LANES - 1:LANES]
        return carry

    lax.fori_loop(0, nkb, p2b, jnp.zeros((tq, 1), F32))

    rep = DSA_HEADS // DSA_KV_HEADS
    res = []
    for g in range(DSA_KV_HEADS):
        in_g = lo_half if g == 0 else jnp.logical_not(lo_half)
        q_all = jnp.concatenate(
            [jnp.where(in_g, aq_ref[:, r * LANES:(r + 1) * LANES], zero) for r in range(rep)], axis=0)

        in_g_k = lax.broadcasted_iota(I32, (kb, LANES), 1) < HEAD_DIM
        if g == 1:
            in_g_k = jnp.logical_not(in_g_k)

        def p3(c, carry):
            m, acc = carry
            off = pl.multiple_of(c * kb, kb)
            lg = _dot_t(q_all, k_ref[0, pl.ds(off, kb), :])
            bias = jnp.concatenate(
                [lax.bitcast_convert_type(s_ref[c * sub + j], F32) for j in range(sub)], axis=1)
            lg = lg + jnp.concatenate([bias] * rep, axis=0)
            mn = jnp.maximum(m, jnp.max(lg, axis=1, keepdims=True))
            a = jnp.exp(m - mn)
            p = jnp.exp(lg - mn)
            vone = jnp.where(in_g_k, v_ref[0, pl.ds(off, kb), :], jnp.ones((), BF16))
            acc = a * acc + _dot(p.astype(BF16), vone)
            return mn, acc

        init = (jnp.full((rep * tq, 1), MASK_VALUE, F32), jnp.zeros((rep * tq, LANES), F32))
        _, acc = lax.fori_loop(0, nkb, p3, init)
        denom = acc[:, HEAD_DIM * (1 - g):HEAD_DIM * (1 - g) + 1]
        res.append(acc / denom)
    for r in range(rep):
        o_ref[:, r * LANES:(r + 1) * LANES] = jnp.where(
            lo_half, res[0][r * tq:(r + 1) * tq], res[1][r * tq:(r + 1) * tq]).astype(BF16)


def _dsa(aq, qi, misc, k_b, v_b, ki_b, tri, *, batch, t, tq, kb, n_keys, q_off, topk):
    lp = k_b.shape[1]
    nq = t // tq
    qrow = lambda w: pl.BlockSpec((tq, w), lambda b, i: (b * nq + i, 0))
    keys = pl.BlockSpec((1, lp, LANES), lambda b, i: (b, 0, 0))
    kern = functools.partial(_dsa_kernel, tq=tq, kb=kb, n_keys=n_keys, q_off=q_off, topk=topk)
    return pl.pallas_call(
        kern,
        grid=(batch, nq),
        in_specs=[qrow(512), qrow(256), qrow(LANES), keys, keys, keys,
                  pl.BlockSpec((LANES, LANES), lambda b, i: (0, 0))],
        out_specs=qrow(512),
        out_shape=jax.ShapeDtypeStruct((batch * t, 512), BF16),
        scratch_shapes=[pltpu.VMEM((lp // LANES, tq, LANES), I32)],
        compiler_params=pltpu.CompilerParams(dimension_semantics=("arbitrary", "arbitrary"),
                                             vmem_limit_bytes=VMEM_LIMIT),
        name="dsa",
    )(aq, qi, misc, k_b, v_b, ki_b, tri)


def _fox_kernel(q_ref, k_ref, v_ref, dq_ref, dk_ref, o_ref, *, tq, kb, n_keys, q_off):
    i = pl.program_id(2)
    nkb = (jnp.minimum(n_keys, q_off + (i + 1) * tq) + kb - 1) // kb
    lane = lax.broadcasted_iota(I32, (tq, LANES), 1)
    lo_half = lane < HEAD_DIM
    zero = jnp.zeros((), BF16)
    q = q_ref[...]
    q_all = jnp.concatenate([jnp.where(lo_half, q, zero), jnp.where(lo_half, zero, q)], axis=0)
    dq = dq_ref[0, 0]
    qpos = q_off + i * tq + lax.broadcasted_iota(I32, (tq, 1), 0)

    def body(c, carry, masked):
        off = pl.multiple_of(c * kb, kb)
        lg = _dot_t(q_all, k_ref[0, pl.ds(off, kb), :])
        vblk = v_ref[0, pl.ds(off, kb), :]
        dk = dk_ref[0, 0, c]
        if masked:
            kpos = off + lax.broadcasted_iota(I32, (1, kb), 1)
            ok = (kpos <= qpos) & (kpos < n_keys)
        out = []
        for e in range(2):
            m, acc = carry[e]
            lge = lg[e * tq:(e + 1) * tq] + dq[:, e:e + 1] - dk[e:e + 1, :]
            if masked:
                lge = jnp.where(ok, lge, MASK_VALUE)
            mn = jnp.maximum(m, jnp.max(lge, axis=1, keepdims=True))
            a = jnp.exp(m - mn)
            p = jnp.exp(lge - mn)
            vone = jnp.where(lo_k if e == 0 else jnp.logical_not(lo_k), vblk, jnp.ones((), BF16))
            acc = a * acc + _dot(p.astype(BF16), vone)
            out.append((mn, acc))
        return tuple(out)

    lo_k = lax.broadcasted_iota(I32, (kb, LANES), 1) < HEAD_DIM
    one = (jnp.full((tq, 1), MASK_VALUE, F32), jnp.zeros((tq, LANES), F32))
    n_full = jnp.minimum((q_off + i * tq + 1) // kb, n_keys // kb)
    carry = lax.fori_loop(0, n_full, functools.partial(body, masked=False), (one, one))
    (_, a0), (_, a1) = lax.fori_loop(n_full, nkb, functools.partial(body, masked=True), carry)
    o_ref[...] = jnp.where(lo_half, a0 / a0[:, HEAD_DIM:HEAD_DIM + 1], a1 / a1[:, 0:1]).astype(BF16)


def _fox(fq, k_b, v_b, dq_p, dk_p, *, batch, t, tq, kb, n_keys, q_off):
    lp = k_b.shape[1]
    nq = t // tq
    npair = FOX_HEADS // 2
    kern = functools.partial(_fox_kernel, tq=tq, kb=kb, n_keys=n_keys, q_off=q_off)
    kv = pl.BlockSpec((1, lp, LANES), lambda b, p, i: (b, 0, p))
    qo = pl.BlockSpec((tq, LANES), lambda b, p, i: (b * nq + i, p))
    return pl.pallas_call(
        kern,
        grid=(batch, npair, nq),
        in_specs=[qo, kv, kv,
                  pl.BlockSpec((1, 1, tq, 2), lambda b, p, i: (b, p, i, 0)),
                  pl.BlockSpec((1, 1, lp // kb, 2, kb), lambda b, p, i: (b, p, 0, 0, 0))],
        out_specs=qo,
        out_shape=jax.ShapeDtypeStruct((batch * t, FOX_HEADS * HEAD_DIM), BF16),
        compiler_params=pltpu.CompilerParams(dimension_semantics=("arbitrary",) * 3,
                                             vmem_limit_bytes=VMEM_LIMIT),
        name="fox",
    )(fq, k_b, v_b, dq_p, dk_p)


def _memkv_kernel(x_ref, g_ref, w_ref, k_ref, v_ref, kb_ref, vb_ref):
    xn = _rms(x_ref[...], g_ref[...]).astype(BF16)
    k = _dot(xn, w_ref[:, :D_MODEL])
    v = _dot(xn, w_ref[:, D_MODEL:])
    k_ref[...] = k
    v_ref[...] = v
    kb_ref[...] = k.astype(BF16)
    vb_ref[...] = v.astype(BF16)


def _memkv(mem2, g, w_kv):
    n = mem2.shape[0]
    row = pl.BlockSpec((N_MEM, D_MODEL), lambda i: (i, 0))
    return pl.pallas_call(
        _memkv_kernel,
        grid=(n // N_MEM,),
        in_specs=[row, pl.BlockSpec((1, D_MODEL), lambda i: (0, 0)),
                  pl.BlockSpec((D_MODEL, 2 * D_MODEL), lambda i: (0, 0))],
        out_specs=[row] * 4,
        out_shape=[jax.ShapeDtypeStruct((n, D_MODEL), dt) for dt in (F32, F32, BF16, BF16)],
        compiler_params=pltpu.CompilerParams(dimension_semantics=("arbitrary",), vmem_limit_bytes=VMEM_LIMIT),
        name="memkv",
    )(mem2, g, w_kv)


def _post_kernel(x_ref, oa_ref, of_ref, woa_ref, wof_ref, gm_ref, wq_ref, mk_ref, mv_ref, wo_ref, gf_ref,
                 h_ref, xn_ref):
    h = x_ref[...] + _dot(oa_ref[...], woa_ref[...]) + _dot(of_ref[...], wof_ref[...])
    hn = _rms(h, gm_ref[...]).astype(BF16)
    q = (_dot(hn, wq_ref[...]) * (MEM_HEAD_DIM ** -0.5)).astype(BF16)
    outs = []
    for hd in range(MEM_HEADS):
        cs = slice(hd * MEM_HEAD_DIM, (hd + 1) * MEM_HEAD_DIM)
        lg = _dot_t(q[:, cs], mk_ref[0, :, cs])
        p = jnp.exp(lg - jnp.max(lg, axis=1, keepdims=True))
        p = p / jnp.sum(p, axis=1, keepdims=True)
        outs.append(_dot(p.astype(BF16), mv_ref[0, :, cs]))
    o = jnp.concatenate(outs, axis=1).astype(BF16)
    h = h + _dot(o, wo_ref[...])
    h_ref[...] = h
    xn_ref[...] = _rms(h, gf_ref[...])


def _post(x2, oa, of, woa, wof, gm, wq, mk_b, mv_b, wo, gf, *, batch, t, tb):
    nt = t // tb
    row = lambda w: pl.BlockSpec((tb, w), lambda b, i: (b * nt + i, 0))
    const = lambda s: pl.BlockSpec(s, lambda b, i: (0, 0))
    mem = pl.BlockSpec((1, N_MEM, D_MODEL), lambda b, i: (b, 0, 0))
    return pl.pallas_call(
        _post_kernel,
        grid=(batch, nt),
        in_specs=[row(D_MODEL), row(512), row(512), const((512, D_MODEL)), const((512, D_MODEL)),
                  const((1, D_MODEL)), const((D_MODEL, D_MODEL)), mem, mem, const((D_MODEL, D_MODEL)),
                  const((1, D_MODEL))],
        out_specs=[row(D_MODEL), row(D_MODEL)],
        out_shape=[jax.ShapeDtypeStruct((batch * t, D_MODEL), F32)] * 2,
        compiler_params=pltpu.CompilerParams(dimension_semantics=("arbitrary", "arbitrary"),
                                             vmem_limit_bytes=VMEM_LIMIT),
        name="post",
    )(x2, oa, of, woa, wof, gm, wq, mk_b, mv_b, wo, gf)


def _top16(s):
    r = s.shape[0]
    row = lax.broadcasted_iota(I32, s.shape, 0).astype(F32)
    vals, ids = [], []
    for _ in range(PEER_TOPK):
        m = jnp.max(s, axis=0, keepdims=True)
        am = jnp.min(jnp.where(s == m, row, float(r)), axis=0, keepdims=True)
        vals.append(m)
        ids.append(am)
        s = jnp.where(row == am, -jnp.inf, s)
    return jnp.concatenate(vals, axis=0), jnp.concatenate(ids, axis=0)


def _peer_sel_kernel(x_ref, wqt_ref, k1_ref, k2_ref, eidx_ref, gate_ref, qt_scr, e_scr, g_scr):
    half = PEER_KEY_DIM // 2
    qt_scr[...] = _dot_t(wqt_ref[...], x_ref[...].astype(BF16)).astype(BF16)
    n = qt_scr.shape[1]
    pad = (-N_CAND) % 8

    def head(h, _):
        r0 = pl.multiple_of(h * PEER_KEY_DIM, PEER_KEY_DIM)
        v1, i1 = _top16(_dot(k1_ref[h], qt_scr[pl.ds(r0, half), :]))
        v2, i2 = _top16(_dot(k2_ref[h], qt_scr[pl.ds(r0 + half, half), :]))
        cand = jnp.concatenate([v1[a:a + 1] + v2[b:b + 1] for a, b in CAND]
                               + [jnp.full((pad, n), -jnp.inf, F32)], axis=0)
        cidx = jnp.concatenate([i1[a:a + 1] * float(N_KEYS) + i2[b:b + 1] for a, b in CAND]
                               + [jnp.zeros((pad, n), F32)], axis=0)
        sv, sr = _top16(cand)
        row = lax.broadcasted_iota(I32, cand.shape, 0).astype(F32)
        eid = jnp.concatenate(
            [jnp.sum(jnp.where(row == sr[k:k + 1], cidx, 0.0), axis=0, keepdims=True) for k in range(PEER_TOPK)],
            axis=0)
        p = jnp.exp(sv - sv[0:1])
        o0 = pl.multiple_of(h * PEER_TOPK, PEER_TOPK)
        e_scr[pl.ds(o0, PEER_TOPK), :] = eid
        g_scr[pl.ds(o0, PEER_TOPK), :] = p / jnp.sum(p, axis=0, keepdims=True)
        return 0

    lax.fori_loop(0, PEER_HEADS, head, 0)
    eidx_ref[...] = e_scr[...].T.astype(I32)
    gate_ref[...] = g_scr[...].T


def _peer_sel(xn2, wq_t, keys1, keys2, tb):
    n = xn2.shape[0]
    kspec = pl.BlockSpec((PEER_HEADS, N_KEYS, PEER_KEY_DIM // 2), lambda i: (0, 0, 0))
    out = pl.BlockSpec((tb, PEER_HEADS * PEER_TOPK), lambda i: (i, 0))
    return pl.pallas_call(
        _peer_sel_kernel,
        grid=(n // tb,),
        in_specs=[pl.BlockSpec((tb, D_MODEL), lambda i: (i, 0)),
                  pl.BlockSpec((PEER_HEADS * PEER_KEY_DIM, D_MODEL), lambda i: (0, 0)), kspec, kspec],
        out_specs=[out, out],
        out_shape=[jax.ShapeDtypeStruct((n, PEER_HEADS * PEER_TOPK), I32),
                   jax.ShapeDtypeStruct((n, PEER_HEADS * PEER_TOPK), F32)],
        scratch_shapes=[pltpu.VMEM((PEER_HEADS * PEER_KEY_DIM, tb), BF16)]
        + [pltpu.VMEM((PEER_HEADS * PEER_TOPK, tb), F32)] * 2,
        compiler_params=pltpu.CompilerParams(dimension_semantics=("arbitrary",), vmem_limit_bytes=VMEM_LIMIT),
        name="peer_sel",
    )(xn2, wq_t, keys1, keys2)


N_PICK = PEER_HEADS * PEER_TOPK
P_AHEAD = 6
N_SLOT = P_AHEAD + 4
GROUP = 8
HI_MASK = -65536


def _gelu(x):
    return 0.5 * x * (1.0 + lax.erf(x * (2.0 ** -0.5)))


def _peer_ffn_kernel(eidx_ref, gate_ref, x_ref, h_ref, fg_ref, ones_ref, eye_ref, uv_ref, y_ref,
                     buf, hd_scr, w_scr, wc_scr, acc, sem, *, tt):
    @pl.when(pl.program_id(0) == 0)
    def _():
        buf[...] = jnp.zeros(buf.shape, I32)
        hd_scr[...] = jnp.zeros(hd_scr.shape, F32)
        w_scr[...] = jnp.zeros(w_scr.shape, F32)
        wc_scr[...] = jnp.zeros(wc_scr.shape, F32)

    n_tiles = D_MODEL // LANES
    n_groups = N_PICK // GROUP
    c1_at, c2_at = 1, 10

    def expert_copy(t, j, slot):
        return pltpu.make_async_copy(uv_ref.at[eidx_ref[t * N_PICK + j]], buf.at[slot, :, j, :], sem.at[slot])

    def packed_rows(slot, rows):
        return jnp.concatenate([buf[slot, s, rows, :] for s in range(n_tiles)], axis=1)

    def lane_tiles_sum(a):
        out = a[:, 0:LANES]
        for c in range(1, n_tiles):
            out = out + a[:, c * LANES:(c + 1) * LANES]
        return out

    def step(t, r, issue, gather_u, mix1, mix2):
        slot_v = (r - 3) % N_SLOT
        par = r % 2
        if gather_u:
            for _ in range(N_PICK):
                pltpu.make_async_copy(uv_ref.at[0], buf.at[r, :, 0, :], sem.at[r]).wait()
            xrow = jnp.broadcast_to(x_ref[pl.ds(t, 1), :], (GROUP, D_MODEL))
        acc8 = jnp.zeros((GROUP, D_MODEL), F32)
        part = w_new = wcol_new = None
        for k in range(n_groups):
            rows = slice(k * GROUP, (k + 1) * GROUP)
            if issue:
                for e in range(GROUP):
                    expert_copy(t + P_AHEAD, k * GROUP + e, (r + P_AHEAD) % N_SLOT).start(priority=e % 2)
            if gather_u:
                if part is not None:
                    hd_scr[par, (k - 1) * GROUP:k * GROUP, :] = part
                u = lax.bitcast_convert_type(packed_rows(r, rows) & HI_MASK, F32)
                part = lane_tiles_sum(u * xrow)
            v = lax.bitcast_convert_type(packed_rows(slot_v, rows) << 16, F32)
            acc8 = acc8 + v * jnp.concatenate([wc_scr[rows, :]] * n_tiles, axis=1)
            if mix1 and k == c1_at:
                hdp = hd_scr[1 - par]
                hi = hdp.astype(BF16)
                lo = (hdp - hi.astype(F32)).astype(BF16)
                ones8 = ones_ref[0:GROUP, :]
                hd = _dot_t(ones8, hi) + _dot_t(ones8, lo)
                w_new = gate_ref[pl.ds(jnp.maximum(t - 1, 0), 1), :] * _gelu(hd)
            if mix2 and k == c2_at:
                diag = (eye_ref[...] * jnp.broadcast_to(w_scr[0:1, :], (LANES, LANES))).astype(BF16)
                wcol_new = _dot(diag, ones_ref[...])
        if gather_u:
            hd_scr[par, (n_groups - 1) * GROUP:, :] = part
        if mix2:
            wc_scr[...] = wcol_new
        if mix1:
            w_scr[...] = w_new
        acc[pl.ds(jnp.maximum(t - 3, 0), 1), :] = jnp.sum(acc8, axis=0, keepdims=True)

    for t0 in range(P_AHEAD):
        for j in range(N_PICK):
            expert_copy(t0, j, t0).start(priority=j % 2)

    n_round = (tt - P_AHEAD) // N_SLOT

    def one_round(q, _):
        for r in range(N_SLOT):
            step(q * N_SLOT + r, r, True, True, True, True)
        return 0

    lax.fori_loop(0, n_round, one_round, 0)
    for t in range(n_round * N_SLOT, tt + 3):
        step(t, t % N_SLOT, t + P_AHEAD < tt, t < tt, t <= tt, t <= tt + 1)
    y_ref[...] = _rms(h_ref[...] + acc[...], fg_ref[...])


def _peer_ffn(eidx, gate, xn2, h2, fg, uv, tt):
    n = xn2.shape[0]
    row = lambda w: pl.BlockSpec((tt, w), lambda i: (i, 0))
    const = lambda s: pl.BlockSpec(s, lambda i: (0, 0))
    return pl.pallas_call(
        functools.partial(_peer_ffn_kernel, tt=tt),
        grid=(n // tt,),
        in_specs=[pl.BlockSpec((tt * N_PICK,), lambda i: (i,), memory_space=pltpu.SMEM),
                  row(N_PICK), row(D_MODEL), row(D_MODEL), const((1, D_MODEL)), const((LANES, LANES)),
                  const((LANES, LANES)), pl.BlockSpec(memory_space=pl.ANY)],
        out_specs=row(D_MODEL),
        out_shape=jax.ShapeDtypeStruct((n, D_MODEL), F32),
        scratch_shapes=[pltpu.VMEM((N_SLOT, D_MODEL // LANES, N_PICK, LANES), I32), pltpu.VMEM((2, N_PICK, LANES), F32),
                        pltpu.VMEM((GROUP, LANES), F32), pltpu.VMEM((N_PICK, LANES), F32),
                        pltpu.VMEM((tt, D_MODEL), F32), pltpu.SemaphoreType.DMA((N_SLOT,))],
        compiler_params=pltpu.CompilerParams(dimension_semantics=("arbitrary",), vmem_limit_bytes=VMEM_LIMIT),
        name="peer_ffn",
    )(eidx.reshape(-1), gate, xn2, h2, fg, jnp.ones((LANES, LANES), BF16), jnp.eye(LANES, dtype=F32), uv)


def _rope_tables(pos):
    inv_freq = ROPE_THETA ** (-jnp.arange(0, ROT_DIM, 2, dtype=F32) / ROT_DIM)
    ang = pos.astype(F32)[:, None] * inv_freq[None, :]
    cos, sin = jnp.cos(ang), jnp.sin(ang)
    n = pos.shape[0]
    rest = HEAD_DIM - ROT_DIM
    hr = ROT_DIM // 2
    c = jnp.concatenate([cos, cos, jnp.ones((n, rest), F32)], axis=1)
    a = jnp.concatenate([-sin, jnp.zeros((n, hr + rest), F32)], axis=1)
    b = jnp.concatenate([jnp.zeros((n, hr), F32), sin, jnp.zeros((n, rest), F32)], axis=1)
    return tuple(jnp.tile(z, (1, LANES // HEAD_DIM)) for z in (c, a, b))


Q_PERM = np.concatenate([np.r_[j * HEAD_DIM:(j + 1) * HEAD_DIM, (4 + j) * HEAD_DIM:(5 + j) * HEAD_DIM]
                         for j in range(4)])


def _pack_uv(u, v):
    bits = lambda a: lax.bitcast_convert_type(a.astype(jnp.bfloat16), jnp.uint16).astype(jnp.uint32)
    return lax.bitcast_convert_type((bits(u) << 16) | bits(v), I32).reshape(-1, D_MODEL // LANES, LANES)


def _prep_weights(norm_mix_g, w_in, b_fox_f, w_out, norm_mem_g, w_mem_q, w_mem_o, norm_ffn_g, peer_w_q,
                  peer_keys1, peer_keys2, peer_u, peer_v):
    sp = np.cumsum([0, 512, 128, 128, 256, 64, 4, 512, 512, 512, 8])
    a_q, a_k, a_v, a_qi, a_ki, a_w, f_q, f_k, f_v, f_f = [w_in[:, sp[i]:sp[i + 1]] for i in range(10)]
    zpad = jnp.zeros((D_MODEL, LANES - IDX_HEADS - FOX_HEADS), F32)
    w_cat = jnp.concatenate([a_q[:, Q_PERM], a_k, a_v, a_qi, a_ki, a_ki, a_w, f_f, zpad, f_q, f_k, f_v],
                            axis=1).astype(BF16)
    b_misc = jnp.concatenate([jnp.zeros((IDX_HEADS,), F32), b_fox_f.astype(F32),
                              jnp.zeros((LANES - IDX_HEADS - FOX_HEADS,), F32)])[None, :]
    r = lambda g: g.astype(F32)[None, :]
    return dict(
        g_mix=r(norm_mix_g), w_cat=w_cat, b_misc=b_misc,
        woa=w_out[:512][Q_PERM].astype(BF16), wof=w_out[512:].astype(BF16),
        g_mem=r(norm_mem_g), wq=w_mem_q.astype(BF16), wo=w_mem_o.astype(BF16), g_ffn=r(norm_ffn_g),
        wq_t=peer_w_q.T.astype(BF16), keys1=peer_keys1.astype(BF16), keys2=peer_keys2.astype(BF16),
        uv=_pack_uv(peer_u, peer_v),
    )


def _round_up(x, m):
    return (x + m - 1) // m * m


def _layer(x, q_off, past, mk_b, mv_b, w, fg, tri, cfg):
    batch, t, _ = x.shape
    n = batch * t
    x2 = x.reshape(n, D_MODEL)
    tabs = _rope_tables(q_off + jnp.arange(t, dtype=I32))
    tb_in = cfg["tb_in"]
    if tb_in > t:
        tabs = tuple(jnp.tile(z, (tb_in // t, 1)) for z in tabs)
    (aq, ak, av, qi, ki, misc, fq, fk, fv, akb, avb, kib, fkb, fvb) = _inproj(
        x2, w["g_mix"], w["w_cat"], w["b_misc"], *tabs, tb_in)
    logf = misc[:, IDX_HEADS:IDX_HEADS + FOX_HEADS].reshape(batch, t, FOX_HEADS)
    new_rows = (ak.reshape(batch, t, DSA_KV_HEADS, HEAD_DIM), av.reshape(batch, t, DSA_KV_HEADS, HEAD_DIM),
                ki[:, :IDX_DIM].reshape(batch, t, IDX_DIM), fk.reshape(batch, t, FOX_HEADS, HEAD_DIM),
                fv.reshape(batch, t, FOX_HEADS, HEAD_DIM), logf)

    kb = cfg["kb"]
    n_past = 0 if past is None else past[0].shape[1]
    n_keys = n_past + t
    lp = _round_up(n_keys, kb)

    def keys(new_b, old, width):
        new_b = new_b.reshape(batch, t, width)
        parts = [new_b] if old is None else [old.reshape(batch, n_past, -1).astype(BF16), new_b]
        if lp > n_keys:
            parts.append(jnp.zeros((batch, lp - n_keys, width), BF16))
        return parts[0] if len(parts) == 1 else jnp.concatenate(parts, axis=1)

    if past is None:
        k_all, v_all, ki_all = keys(akb, None, LANES), keys(avb, None, LANES), keys(kib, None, LANES)
        fk_all, fv_all = keys(fkb, None, 512), keys(fvb, None, 512)
        logf_all = logf
    else:
        p_k, p_v, p_ki, p_fk, p_fv, p_logf = past
        ki_dup = jnp.concatenate([p_ki, p_ki], axis=-1)
        k_all, v_all, ki_all = keys(akb, p_k, LANES), keys(avb, p_v, LANES), keys(kib, ki_dup, LANES)
        fk_all, fv_all = keys(fkb, p_fk, 512), keys(fvb, p_fv, 512)
        logf_all = jnp.concatenate([p_logf.astype(F32), logf], axis=1)

    lf = jnp.pad(logf_all, ((0, 0), (0, lp - n_keys), (0, 0)))
    lf_t = lf.reshape(batch, lp // LANES, LANES, FOX_HEADS).transpose(0, 1, 3, 2)
    d_t = _cumsum(lf_t, tri).transpose(0, 2, 1, 3).reshape(batch, FOX_HEADS, lp)
    npair = FOX_HEADS // 2
    dq_p = d_t[:, :, n_keys - t:n_keys].reshape(batch, npair, 2, t).transpose(0, 1, 3, 2)
    dk_p = d_t.reshape(batch, npair, 2, lp // kb, kb).transpose(0, 1, 3, 2, 4)

    topk = min(TOPK_MAX, n_keys // 4)
    oa = _dsa(aq, qi, misc, k_all, v_all, ki_all, tri, batch=batch, t=t, tq=cfg["tq_dsa"], kb=kb,
              n_keys=n_keys, q_off=q_off, topk=topk)
    of = _fox(fq, fk_all, fv_all, dq_p, dk_p, batch=batch, t=t, tq=cfg["tq_fox"], kb=kb,
              n_keys=n_keys, q_off=q_off)
    h2, xn2 = _post(x2, oa, of, w["woa"], w["wof"], w["g_mem"], w["wq"], mk_b, mv_b, w["wo"], w["g_ffn"],
                    batch=batch, t=t, tb=cfg["tb_post"])
    eidx, gate = _peer_sel(xn2, w["wq_t"], w["keys1"], w["keys2"], cfg["tb_sel"])
    y = _peer_ffn(eidx, gate, xn2, h2, fg, w["uv"], cfg["tt_ffn"])
    return y.reshape(batch, t, D_MODEL), new_rows


def _config(t, n_tokens, has_past):
    if has_past:
        return dict(tb_in=n_tokens, kb=384, tq_dsa=t, tq_fox=t, tb_post=t, tb_sel=n_tokens, tt_ffn=min(128, n_tokens))
    return dict(tb_in=min(512, t), kb=min(1024, t), tq_dsa=min(128, t), tq_fox=min(1024, t), tb_post=min(256, t),
                tb_sel=min(256, n_tokens), tt_ffn=min(256, n_tokens))


def kernel(x_prompt, x_sample, cache_dsa_k, cache_dsa_v, cache_dsa_kidx, cache_fox_k, cache_fox_v, cache_fox_logf, cache_mem_k, cache_mem_v, mem_prompt, norm_mix_g, w_in, b_fox_f, w_out, mem_norm_g, w_mem_kv, norm_mem_g, w_mem_q, w_mem_o, norm_ffn_g, peer_w_q, peer_keys1, peer_keys2, peer_u, peer_v, final_norm_g):
    depth = w_in.shape[0]
    assert depth == 1, "single-layer trunk"
    l = 0
    bp, tp, _ = x_prompt.shape
    bs, ts, _ = x_sample.shape
    past_len = cache_dsa_k.shape[2]
    w = _prep_weights(norm_mix_g[l], w_in[l], b_fox_f[l], w_out[l], norm_mem_g[l], w_mem_q[l], w_mem_o[l],
                      norm_ffn_g[l], peer_w_q[l], peer_keys1[l], peer_keys2[l], peer_u[l], peer_v[l])
    fg = final_norm_g.astype(F32)[None, :]
    ii = np.arange(LANES)
    tri = jnp.asarray(ii[:, None] <= ii[None, :], BF16)

    mk, mv, mkb, mvb = _memkv(mem_prompt.reshape(bp * N_MEM, D_MODEL), mem_norm_g[l].astype(F32)[None, :],
                              w_mem_kv[l].astype(BF16))
    y_p, new_p = _layer(x_prompt, 0, None, mkb.reshape(bp, N_MEM, D_MODEL), mvb.reshape(bp, N_MEM, D_MODEL),
                        w, fg, tri, _config(tp, bp * tp, False))
    past = (cache_dsa_k[l], cache_dsa_v[l], cache_dsa_kidx[l], cache_fox_k[l], cache_fox_v[l], cache_fox_logf[l])
    y_s, new_s = _layer(x_sample, past_len, past, cache_mem_k[l].reshape(bs, N_MEM, D_MODEL).astype(BF16),
                        cache_mem_v[l].reshape(bs, N_MEM, D_MODEL).astype(BF16), w, fg, tri,
                        _config(ts, bs * ts, True))
    st = lambda a: a[None]
    mem_shape = (bp, N_MEM, MEM_HEADS, MEM_HEAD_DIM)
    return (y_p, y_s) + tuple(st(a) for a in new_p) + (st(mk.reshape(mem_shape)), st(mv.reshape(mem_shape))) \
        + tuple(st(a) for a in new_s)
```

```python
import functools

import jax
import jax.numpy as jnp
import numpy as np
from jax import lax
from jax.experimental import pallas as pl
from jax.experimental.pallas import tpu as pltpu

D_MODEL = 1024
CHUNK_SHIFT = 6
DSA_HEADS = 8
FOX_HEADS = 8
HEAD_DIM = 64
DSA_KV_HEADS = 2
IDX_HEADS = 4
IDX_DIM = 64
TOPK_MAX = 256
ROPE_THETA = 500000.0
ROT_DIM = HEAD_DIM // 4
N_MEM = 256
MEM_HEADS = 4
MEM_HEAD_DIM = D_MODEL // MEM_HEADS
PEER_HEADS = 8
PEER_KEY_DIM = 256
N_KEYS = 128
PEER_TOPK = 16
EPS = 1e-6
LANES = 128
VMEM_LIMIT = 48 * 1024 * 1024

F32 = jnp.float32
BF16 = jnp.bfloat16
I32 = jnp.int32

INT_MIN = int(np.iinfo(np.int32).min)
_NEG_INF_BITS = int(np.array(-np.inf, np.float32).view(np.int32))
KEY_NEG_INF = _NEG_INF_BITS ^ 0x7FFFFFFF
KEY_NEG_INF = KEY_NEG_INF - (1 << 32) if KEY_NEG_INF >= (1 << 31) else KEY_NEG_INF
MASK_VALUE = -1e30
MASK_BITS = int(np.array(MASK_VALUE, np.float32).view(np.int32))

SEG_Q = (0, 512)
SEG_K = (512, 640)
SEG_V = (640, 768)
SEG_QI = (768, 1024)
SEG_KI = (1024, 1152)
SEG_MISC = (1152, 1280)
SEG_FQ = (1280, 1792)
SEG_FK = (1792, 2304)
SEG_FV = (2304, 2816)
IN_COLS = 2816

CAND = [(a, b) for a in range(PEER_TOPK) for b in range(PEER_TOPK) if (a + 1) * (b + 1) <= PEER_TOPK]
N_CAND = len(CAND)


def _dot(a, b):
    return jnp.dot(a, b, preferred_element_type=F32)


def _dot_t(a, b):
    return lax.dot_general(a, b, (((1,), (1,)), ((), ())), preferred_element_type=F32)


def _rms(x, g):
    return x * lax.rsqrt(jnp.mean(x * x, axis=-1, keepdims=True) + EPS) * g


def _inproj_kernel(x_ref, g_ref, w_ref, b_ref, cos_ref, sa_ref, sb_ref,
                   aq_ref, ak_ref, av_ref, qi_ref, ki_ref, misc_ref, fq_ref, fk_ref, fv_ref,
                   akb_ref, avb_ref, kib_ref, fkb_ref, fvb_ref):
    xn = _rms(x_ref[...], g_ref[...]).astype(BF16)
    cos, sa, sb = cos_ref[...], sa_ref[...], sb_ref[...]

    def proj(seg):
        return _dot(xn, w_ref[:, seg[0]:seg[1]])

    def rope(y):
        outs = []
        for j in range(y.shape[1] // LANES):
            s = y[:, j * LANES:(j + 1) * LANES]
            outs.append(s * cos + pltpu.roll(s, LANES - ROT_DIM // 2, 1) * sa
                        + pltpu.roll(s, ROT_DIM // 2, 1) * sb)
        return outs[0] if len(outs) == 1 else jnp.concatenate(outs, axis=1)

    scale = HEAD_DIM ** -0.5
    aq_ref[...] = (rope(proj(SEG_Q)) * scale).astype(BF16)
    ak = rope(proj(SEG_K))
    ak_ref[...] = ak
    akb_ref[...] = ak.astype(BF16)
    av = proj(SEG_V)
    av_ref[...] = av
    avb_ref[...] = av.astype(BF16)
    qi_ref[...] = rope(proj(SEG_QI)).astype(BF16)
    ki = rope(proj(SEG_KI))
    ki_ref[...] = ki
    kib_ref[...] = ki.astype(BF16)
    z = proj(SEG_MISC)
    zb = z + b_ref[...]
    logf = jnp.minimum(zb, 0.0) - jnp.log1p(jnp.exp(-jnp.abs(zb)))
    lane = lax.broadcasted_iota(I32, z.shape, 1)
    misc_ref[...] = jnp.where((lane >= IDX_HEADS) & (lane < IDX_HEADS + FOX_HEADS), logf, z)
    fq_ref[...] = (proj(SEG_FQ) * scale).astype(BF16)
    fk = proj(SEG_FK)
    fk_ref[...] = fk
    fkb_ref[...] = fk.astype(BF16)
    fv = proj(SEG_FV)
    fv_ref[...] = fv
    fvb_ref[...] = fv.astype(BF16)


def _inproj(x2, g, w_cat, b_misc, cos, sa, sb, tb):
    n = x2.shape[0]
    ntab = cos.shape[0] // tb
    row = lambda w: pl.BlockSpec((tb, w), lambda i: (i, 0))
    tab = pl.BlockSpec((tb, LANES), lambda i: (i % ntab, 0))
    const = lambda s: pl.BlockSpec(s, lambda i: (0, 0))
    widths = [(512, BF16), (128, F32), (128, F32), (256, BF16), (128, F32), (128, F32),
              (512, BF16), (512, F32), (512, F32),
              (128, BF16), (128, BF16), (128, BF16), (512, BF16), (512, BF16)]
    return pl.pallas_call(
        _inproj_kernel,
        grid=(n // tb,),
        in_specs=[row(D_MODEL), const((1, D_MODEL)), const((D_MODEL, IN_COLS)), const((1, LANES)), tab, tab, tab],
        out_specs=[row(w) for w, _ in widths],
        out_shape=[jax.ShapeDtypeStruct((n, w), dt) for w, dt in widths],
        compiler_params=pltpu.CompilerParams(dimension_semantics=("arbitrary",), vmem_limit_bytes=VMEM_LIMIT),
        name="inproj",
    )(x2, g, w_cat, b_misc, cos, sa, sb)


def _split3(x):
    hi = x.astype(BF16)
    r1 = x - hi.astype(F32)
    mid = r1.astype(BF16)
    lo = (r1 - mid.astype(F32)).astype(BF16)
    return hi, mid, lo


def _cumsum_kernel(x_ref, tri_ref, o_ref):
    tri = tri_ref[...]

    def body(c, carry):
        hi, mid, lo = _split3(x_ref[0, c])
        out = _dot(hi, tri) + _dot(mid, tri) + _dot(lo, tri) + carry
        o_ref[0, c] = out
        return out[:, LANES - 1:LANES]

    lax.fori_loop(0, x_ref.shape[1], body, jnp.zeros((FOX_HEADS, 1), F32))


def _cumsum(logf_t, tri):
    b, nchunk = logf_t.shape[:2]
    spec = pl.BlockSpec((1, nchunk, FOX_HEADS, LANES), lambda i: (i, 0, 0, 0))
    return pl.pallas_call(
        _cumsum_kernel,
        grid=(b,),
        in_specs=[spec, pl.BlockSpec((LANES, LANES), lambda i: (0, 0))],
        out_specs=spec,
        out_shape=jax.ShapeDtypeStruct(logf_t.shape, F32),
        name="cumsum",
    )(logf_t, tri)


def _dsa_kernel(aq_ref, qi_ref, misc_ref, k_ref, v_ref, ki_ref, tri_ref, o_ref, s_ref,
                *, tq, kb, n_keys, q_off, topk):
    i = pl.program_id(1)
    sub = kb // LANES
    q_last = q_off + (i + 1) * tq - 1
    adm_len = jnp.minimum(n_keys, ((q_last >> CHUNK_SHIFT) + 1) << CHUNK_SHIFT)
    nkb = (adm_len + kb - 1) // kb
    qchunk = (q_off + i * tq + lax.broadcasted_iota(I32, (tq, 1), 0)) >> CHUNK_SHIFT
    lane = lax.broadcasted_iota(I32, (tq, LANES), 1)
    lo_half = lane < HEAD_DIM

    zero = jnp.zeros((), BF16)
    qh = []
    for j in range(IDX_HEADS // 2):
        slab = qi_ref[:, j * LANES:(j + 1) * LANES]
        qh += [jnp.where(lo_half, slab, zero), jnp.where(lo_half, zero, slab)]
    wi = [misc_ref[:, h:h + 1] for h in range(IDX_HEADS)]

    def p1(c, _):
        off = pl.multiple_of(c * kb, kb)
        kib = ki_ref[0, pl.ds(off, kb), :]
        sc = jnp.zeros((tq, kb), F32)
        for h in range(IDX_HEADS):
            sc = sc + jnp.maximum(_dot_t(qh[h], kib), 0.0) * wi[h]
        kpos = off + lax.broadcasted_iota(I32, (1, kb), 1)
        ok = ((kpos >> CHUNK_SHIFT) <= qchunk) & (kpos < n_keys)
        bits = lax.bitcast_convert_type(sc, I32)
        key = jnp.where(ok, bits ^ ((bits >> 31) & 0x7FFFFFFF), KEY_NEG_INF)
        for j in range(sub):
            s_ref[c * sub + j] = key[:, j * LANES:(j + 1) * LANES]
        return 0

    lax.fori_loop(0, nkb, p1, 0)

    def count_ge(trial):
        tb = jnp.broadcast_to(trial, (tq, LANES))

        def body(c, acc):
            for j in range(sub):
                acc = acc + (s_ref[c * sub + j] >= tb).astype(I32)
            return acc

        acc = lax.fori_loop(0, nkb, body, jnp.zeros((tq, LANES), I32))
        return jnp.sum(acc, axis=1, keepdims=True)

    def bisect(it, cand):
        trial = cand | (jnp.int32(1) << (31 - it))
        cnt = count_ge(trial ^ INT_MIN)
        return jnp.where(cnt >= topk, trial, cand)

    vstar = lax.fori_loop(0, 32, bisect, jnp.zeros((tq, 1), I32)) ^ INT_MIN
    need = (topk - count_ge(vstar + 1)).astype(F32)

    vb = jnp.broadcast_to(vstar, (tq, LANES))
    tri = tri_ref[...]

    def p2b(c, carry):
        for j in range(sub):
            blk = s_ref[c * sub + j]
            eq = blk == vb
            pref = _dot(jnp.where(eq, 1.0, 0.0).astype(BF16), tri)
            sel = ((blk > vb) | (eq & (pref + carry <= need))) & (blk > KEY_NEG_INF)
            s_ref[c * sub + j] = jnp.where(sel, 0, MASK_BITS)
            carry = carry + pref[:, LANES - 1:LANES]
        return carry

    lax.fori_loop(0, nkb, p2b, jnp.zeros((tq, 1), F32))

    rep = DSA_HEADS // DSA_KV_HEADS
    res = []
    for g in range(DSA_KV_HEADS):
        in_g = lo_half if g == 0 else jnp.logical_not(lo_half)
        q_all = jnp.concatenate(
            [jnp.where(in_g, aq_ref[:, r * LANES:(r + 1) * LANES], zero) for r in range(rep)], axis=0)

        in_g_k = lax.broadcasted_iota(I32, (kb, LANES), 1) < HEAD_DIM
        if g == 1:
            in_g_k = jnp.logical_not(in_g_k)

        def p3(c, carry):
            m, acc = carry
            off = pl.multiple_of(c * kb, kb)
            lg = _dot_t(q_all, k_ref[0, pl.ds(off, kb), :])
            bias = jnp.concatenate(
                [lax.bitcast_convert_type(s_ref[c * sub + j], F32) for j in range(sub)], axis=1)
            lg = lg + jnp.concatenate([bias] * rep, axis=0)
            mn = jnp.maximum(m, jnp.max(lg, axis=1, keepdims=True))
            a = jnp.exp(m - mn)
            p = jnp.exp(lg - mn)
            vone = jnp.where(in_g_k, v_ref[0, pl.ds(off, kb), :], jnp.ones((), BF16))
            acc = a * acc + _dot(p.astype(BF16), vone)
            return mn, acc

        init = (jnp.full((rep * tq, 1), MASK_VALUE, F32), jnp.zeros((rep * tq, LANES), F32))
        _, acc = lax.fori_loop(0, nkb, p3, init)
        denom = acc[:, HEAD_DIM * (1 - g):HEAD_DIM * (1 - g) + 1]
        res.append(acc / denom)
    for r in range(rep):
        o_ref[:, r * LANES:(r + 1) * LANES] = jnp.where(
            lo_half, res[0][r * tq:(r + 1) * tq], res[1][r * tq:(r + 1) * tq]).astype(BF16)


def _dsa(aq, qi, misc, k_b, v_b, ki_b, tri, *, batch, t, tq, kb, n_keys, q_off, topk):
    lp = k_b.shape[1]
    nq = t // tq
    qrow = lambda w: pl.BlockSpec((tq, w), lambda b, i: (b * nq + i, 0))
    keys = pl.BlockSpec((1, lp, LANES), lambda b, i: (b, 0, 0))
    kern = functools.partial(_dsa_kernel, tq=tq, kb=kb, n_keys=n_keys, q_off=q_off, topk=topk)
    return pl.pallas_call(
        kern,
        grid=(batch, nq),
        in_specs=[qrow(512), qrow(256), qrow(LANES), keys, keys, keys,
                  pl.BlockSpec((LANES, LANES), lambda b, i: (0, 0))],
        out_specs=qrow(512),
        out_shape=jax.ShapeDtypeStruct((batch * t, 512), BF16),
        scratch_shapes=[pltpu.VMEM((lp // LANES, tq, LANES), I32)],
        compiler_params=pltpu.CompilerParams(dimension_semantics=("arbitrary", "arbitrary"),
                                             vmem_limit_bytes=VMEM_LIMIT),
        name="dsa",
    )(aq, qi, misc, k_b, v_b, ki_b, tri)


def _fox_kernel(q_ref, k_ref, v_ref, dq_ref, dk_ref, o_ref, *, tq, kb, n_keys, q_off):
    i = pl.program_id(2)
    nkb = (jnp.minimum(n_keys, q_off + (i + 1) * tq) + kb - 1) // kb
    lane = lax.broadcasted_iota(I32, (tq, LANES), 1)
    lo_half = lane < HEAD_DIM
    zero = jnp.zeros((), BF16)
    q = q_ref[...]
    q_all = jnp.concatenate([jnp.where(lo_half, q, zero), jnp.where(lo_half, zero, q)], axis=0)
    dq = dq_ref[0, 0]
    qpos = q_off + i * tq + lax.broadcasted_iota(I32, (tq, 1), 0)

    def body(c, carry, masked):
        off = pl.multiple_of(c * kb, kb)
        lg = _dot_t(q_all, k_ref[0, pl.ds(off, kb), :])
        vblk = v_ref[0, pl.ds(off, kb), :]
        dk = dk_ref[0, 0, c]
        if masked:
            kpos = off + lax.broadcasted_iota(I32, (1, kb), 1)
            ok = (kpos <= qpos) & (kpos < n_keys)
        out = []
        for e in range(2):
            m, acc = carry[e]
            lge = lg[e * tq:(e + 1) * tq] + dq[:, e:e + 1] - dk[e:e + 1, :]
            if masked:
                lge = jnp.where(ok, lge, MASK_VALUE)
            mn = jnp.maximum(m, jnp.max(lge, axis=1, keepdims=True))
            a = jnp.exp(m - mn)
            p = jnp.exp(lge - mn)
            vone = jnp.where(lo_k if e == 0 else jnp.logical_not(lo_k), vblk, jnp.ones((), BF16))
            acc = a * acc + _dot(p.astype(BF16), vone)
            out.append((mn, acc))
        return tuple(out)

    lo_k = lax.broadcasted_iota(I32, (kb, LANES), 1) < HEAD_DIM
    one = (jnp.full((tq, 1), MASK_VALUE, F32), jnp.zeros((tq, LANES), F32))
    n_full = jnp.minimum((q_off + i * tq + 1) // kb, n_keys // kb)
    carry = lax.fori_loop(0, n_full, functools.partial(body, masked=False), (one, one))
    (_, a0), (_, a1) = lax.fori_loop(n_full, nkb, functools.partial(body, masked=True), carry)
    o_ref[...] = jnp.where(lo_half, a0 / a0[:, HEAD_DIM:HEAD_DIM + 1], a1 / a1[:, 0:1]).astype(BF16)


def _fox(fq, k_b, v_b, dq_p, dk_p, *, batch, t, tq, kb, n_keys, q_off):
    lp = k_b.shape[1]
    nq = t // tq
    npair = FOX_HEADS // 2
    kern = functools.partial(_fox_kernel, tq=tq, kb=kb, n_keys=n_keys, q_off=q_off)
    kv = pl.BlockSpec((1, lp, LANES), lambda b, p, i: (b, 0, p))
    qo = pl.BlockSpec((tq, LANES), lambda b, p, i: (b * nq + i, p))
    return pl.pallas_call(
        kern,
        grid=(batch, npair, nq),
        in_specs=[qo, kv, kv,
                  pl.BlockSpec((1, 1, tq, 2), lambda b, p, i: (b, p, i, 0)),
                  pl.BlockSpec((1, 1, lp // kb, 2, kb), lambda b, p, i: (b, p, 0, 0, 0))],
        out_specs=qo,
        out_shape=jax.ShapeDtypeStruct((batch * t, FOX_HEADS * HEAD_DIM), BF16),
        compiler_params=pltpu.CompilerParams(dimension_semantics=("arbitrary",) * 3,
                                             vmem_limit_bytes=VMEM_LIMIT),
        name="fox",
    )(fq, k_b, v_b, dq_p, dk_p)


def _memkv_kernel(x_ref, g_ref, w_ref, k_ref, v_ref, kb_ref, vb_ref):
    xn = _rms(x_ref[...], g_ref[...]).astype(BF16)
    k = _dot(xn, w_ref[:, :D_MODEL])
    v = _dot(xn, w_ref[:, D_MODEL:])
    k_ref[...] = k
    v_ref[...] = v
    kb_ref[...] = k.astype(BF16)
    vb_ref[...] = v.astype(BF16)


def _memkv(mem2, g, w_kv):
    n = mem2.shape[0]
    row = pl.BlockSpec((N_MEM, D_MODEL), lambda i: (i, 0))
    return pl.pallas_call(
        _memkv_kernel,
        grid=(n // N_MEM,),
        in_specs=[row, pl.BlockSpec((1, D_MODEL), lambda i: (0, 0)),
                  pl.BlockSpec((D_MODEL, 2 * D_MODEL), lambda i: (0, 0))],
        out_specs=[row] * 4,
        out_shape=[jax.ShapeDtypeStruct((n, D_MODEL), dt) for dt in (F32, F32, BF16, BF16)],
        compiler_params=pltpu.CompilerParams(dimension_semantics=("arbitrary",), vmem_limit_bytes=VMEM_LIMIT),
        name="memkv",
    )(mem2, g, w_kv)


def _post_kernel(x_ref, oa_ref, of_ref, woa_ref, wof_ref, gm_ref, wq_ref, mk_ref, mv_ref, wo_ref, gf_ref,
                 h_ref, xn_ref):
    h = x_ref[...] + _dot(oa_ref[...], woa_ref[...]) + _dot(of_ref[...], wof_ref[...])
    hn = _rms(h, gm_ref[...]).astype(BF16)
    q = (_dot(hn, wq_ref[...]) * (MEM_HEAD_DIM ** -0.5)).astype(BF16)
    outs = []
    for hd in range(MEM_HEADS):
        cs = slice(hd * MEM_HEAD_DIM, (hd + 1) * MEM_HEAD_DIM)
        lg = _dot_t(q[:, cs], mk_ref[0, :, cs])
        p = jnp.exp(lg - jnp.max(lg, axis=1, keepdims=True))
        p = p / jnp.sum(p, axis=1, keepdims=True)
        outs.append(_dot(p.astype(BF16), mv_ref[0, :, cs]))
    o = jnp.concatenate(outs, axis=1).astype(BF16)
    h = h + _dot(o, wo_ref[...])
    h_ref[...] = h
    xn_ref[...] = _rms(h, gf_ref[...])


def _post(x2, oa, of, woa, wof, gm, wq, mk_b, mv_b, wo, gf, *, batch, t, tb):
    nt = t // tb
    row = lambda w: pl.BlockSpec((tb, w), lambda b, i: (b * nt + i, 0))
    const = lambda s: pl.BlockSpec(s, lambda b, i: (0, 0))
    mem = pl.BlockSpec((1, N_MEM, D_MODEL), lambda b, i: (b, 0, 0))
    return pl.pallas_call(
        _post_kernel,
        grid=(batch, nt),
        in_specs=[row(D_MODEL), row(512), row(512), const((512, D_MODEL)), const((512, D_MODEL)),
                  const((1, D_MODEL)), const((D_MODEL, D_MODEL)), mem, mem, const((D_MODEL, D_MODEL)),
                  const((1, D_MODEL))],
        out_specs=[row(D_MODEL), row(D_MODEL)],
        out_shape=[jax.ShapeDtypeStruct((batch * t, D_MODEL), F32)] * 2,
        compiler_params=pltpu.CompilerParams(dimension_semantics=("arbitrary", "arbitrary"),
                                             vmem_limit_bytes=VMEM_LIMIT),
        name="post",
    )(x2, oa, of, woa, wof, gm, wq, mk_b, mv_b, wo, gf)


def _top16(s):
    r = s.shape[0]
    row = lax.broadcasted_iota(I32, s.shape, 0).astype(F32)
    vals, ids = [], []
    for _ in range(PEER_TOPK):
        m = jnp.max(s, axis=0, keepdims=True)
        am = jnp.min(jnp.where(s == m, row, float(r)), axis=0, keepdims=True)
        vals.append(m)
        ids.append(am)
        s = jnp.where(row == am, -jnp.inf, s)
    return jnp.concatenate(vals, axis=0), jnp.concatenate(ids, axis=0)


def _peer_sel_kernel(x_ref, wqt_ref, k1_ref, k2_ref, eidx_ref, gate_ref, qt_scr, e_scr, g_scr):
    half = PEER_KEY_DIM // 2
    qt_scr[...] = _dot_t(wqt_ref[...], x_ref[...].astype(BF16)).astype(BF16)
    n = qt_scr.shape[1]
    pad = (-N_CAND) % 8

    def head(h, _):
        r0 = pl.multiple_of(h * PEER_KEY_DIM, PEER_KEY_DIM)
        v1, i1 = _top16(_dot(k1_ref[h], qt_scr[pl.ds(r0, half), :]))
        v2, i2 = _top16(_dot(k2_ref[h], qt_scr[pl.ds(r0 + half, half), :]))
        cand = jnp.concatenate([v1[a:a + 1] + v2[b:b + 1] for a, b in CAND]
                               + [jnp.full((pad, n), -jnp.inf, F32)], axis=0)
        cidx = jnp.concatenate([i1[a:a + 1] * float(N_KEYS) + i2[b:b + 1] for a, b in CAND]
                               + [jnp.zeros((pad, n), F32)], axis=0)
        sv, sr = _top16(cand)
        row = lax.broadcasted_iota(I32, cand.shape, 0).astype(F32)
        eid = jnp.concatenate(
            [jnp.sum(jnp.where(row == sr[k:k + 1], cidx, 0.0), axis=0, keepdims=True) for k in range(PEER_TOPK)],
            axis=0)
        p = jnp.exp(sv - sv[0:1])
        o0 = pl.multiple_of(h * PEER_TOPK, PEER_TOPK)
        e_scr[pl.ds(o0, PEER_TOPK), :] = eid
        g_scr[pl.ds(o0, PEER_TOPK), :] = p / jnp.sum(p, axis=0, keepdims=True)
        return 0

    lax.fori_loop(0, PEER_HEADS, head, 0)
    eidx_ref[...] = e_scr[...].T.astype(I32)
    gate_ref[...] = g_scr[...].T


def _peer_sel(xn2, wq_t, keys1, keys2, tb):
    n = xn2.shape[0]
    kspec = pl.BlockSpec((PEER_HEADS, N_KEYS, PEER_KEY_DIM // 2), lambda i: (0, 0, 0))
    out = pl.BlockSpec((tb, PEER_HEADS * PEER_TOPK), lambda i: (i, 0))
    return pl.pallas_call(
        _peer_sel_kernel,
        grid=(n // tb,),
        in_specs=[pl.BlockSpec((tb, D_MODEL), lambda i: (i, 0)),
                  pl.BlockSpec((PEER_HEADS * PEER_KEY_DIM, D_MODEL), lambda i: (0, 0)), kspec, kspec],
        out_specs=[out, out],
        out_shape=[jax.ShapeDtypeStruct((n, PEER_HEADS * PEER_TOPK), I32),
                   jax.ShapeDtypeStruct((n, PEER_HEADS * PEER_TOPK), F32)],
        scratch_shapes=[pltpu.VMEM((PEER_HEADS * PEER_KEY_DIM, tb), BF16)]
        + [pltpu.VMEM((PEER_HEADS * PEER_TOPK, tb), F32)] * 2,
        compiler_params=pltpu.CompilerParams(dimension_semantics=("arbitrary",), vmem_limit_bytes=VMEM_LIMIT),
        name="peer_sel",
    )(xn2, wq_t, keys1, keys2)


N_PICK = PEER_HEADS * PEER_TOPK
P_AHEAD = 6
N_SLOT = P_AHEAD + 4
GROUP = 8
HI_MASK = -65536


def _gelu(x):
    return 0.5 * x * (1.0 + lax.erf(x * (2.0 ** -0.5)))


def _peer_ffn_kernel(eidx_ref, gate_ref, x_ref, h_ref, fg_ref, ones_ref, eye_ref, uv_ref, y_ref,
                     buf, hd_scr, w_scr, wc_scr, acc, sem, *, tt):
    @pl.when(pl.program_id(0) == 0)
    def _():
        buf[...] = jnp.zeros(buf.shape, I32)
        hd_scr[...] = jnp.zeros(hd_scr.shape, F32)
        w_scr[...] = jnp.zeros(w_scr.shape, F32)
        wc_scr[...] = jnp.zeros(wc_scr.shape, F32)

    n_tiles = D_MODEL // LANES
    n_groups = N_PICK // GROUP
    c1_at, c2_at = 1, 10

    def expert_copy(t, j, slot):
        return pltpu.make_async_copy(uv_ref.at[eidx_ref[t * N_PICK + j]], buf.at[slot, :, j, :], sem.at[slot])

    def packed_rows(slot, rows):
        return jnp.concatenate([buf[slot, s, rows, :] for s in range(n_tiles)], axis=1)

    def lane_tiles_sum(a):
        out = a[:, 0:LANES]
        for c in range(1, n_tiles):
            out = out + a[:, c * LANES:(c + 1) * LANES]
        return out

    def step(t, r, issue, gather_u, mix1, mix2):
        slot_v = (r - 3) % N_SLOT
        par = r % 2
        if gather_u:
            for _ in range(N_PICK):
                pltpu.make_async_copy(uv_ref.at[0], buf.at[r, :, 0, :], sem.at[r]).wait()
            xrow = jnp.broadcast_to(x_ref[pl.ds(t, 1), :], (GROUP, D_MODEL))
        acc8 = jnp.zeros((GROUP, D_MODEL), F32)
        part = w_new = wcol_new = None
        for k in range(n_groups):
            rows = slice(k * GROUP, (k + 1) * GROUP)
            if issue:
                for e in range(GROUP):
                    expert_copy(t + P_AHEAD, k * GROUP + e, (r + P_AHEAD) % N_SLOT).start(priority=e % 2)
            if gather_u:
                if part is not None:
                    hd_scr[par, (k - 1) * GROUP:k * GROUP, :] = part
                u = lax.bitcast_convert_type(packed_rows(r, rows) & HI_MASK, F32)
                part = lane_tiles_sum(u * xrow)
            v = lax.bitcast_convert_type(packed_rows(slot_v, rows) << 16, F32)
            acc8 = acc8 + v * jnp.concatenate([wc_scr[rows, :]] * n_tiles, axis=1)
            if mix1 and k == c1_at:
                hdp = hd_scr[1 - par]
                hi = hdp.astype(BF16)
                lo = (hdp - hi.astype(F32)).astype(BF16)
                ones8 = ones_ref[0:GROUP, :]
                hd = _dot_t(ones8, hi) + _dot_t(ones8, lo)
                w_new = gate_ref[pl.ds(jnp.maximum(t - 1, 0), 1), :] * _gelu(hd)
            if mix2 and k == c2_at:
                diag = (eye_ref[...] * jnp.broadcast_to(w_scr[0:1, :], (LANES, LANES))).astype(BF16)
                wcol_new = _dot(diag, ones_ref[...])
        if gather_u:
            hd_scr[par, (n_groups - 1) * GROUP:, :] = part
        if mix2:
            wc_scr[...] = wcol_new
        if mix1:
            w_scr[...] = w_new
        acc[pl.ds(jnp.maximum(t - 3, 0), 1), :] = jnp.sum(acc8, axis=0, keepdims=True)

    for t0 in range(P_AHEAD):
        for j in range(N_PICK):
            expert_copy(t0, j, t0).start(priority=j % 2)

    n_round = (tt - P_AHEAD) // N_SLOT

    def one_round(q, _):
        for r in range(N_SLOT):
            step(q * N_SLOT + r, r, True, True, True, True)
        return 0

    lax.fori_loop(0, n_round, one_round, 0)
    for t in range(n_round * N_SLOT, tt + 3):
        step(t, t % N_SLOT, t + P_AHEAD < tt, t < tt, t <= tt, t <= tt + 1)
    y_ref[...] = _rms(h_ref[...] + acc[...], fg_ref[...])


def _peer_ffn(eidx, gate, xn2, h2, fg, uv, tt):
    n = xn2.shape[0]
    row = lambda w: pl.BlockSpec((tt, w), lambda i: (i, 0))
    const = lambda s: pl.BlockSpec(s, lambda i: (0, 0))
    return pl.pallas_call(
        functools.partial(_peer_ffn_kernel, tt=tt),
        grid=(n // tt,),
        in_specs=[pl.BlockSpec((tt * N_PICK,), lambda i: (i,), memory_space=pltpu.SMEM),
                  row(N_PICK), row(D_MODEL), row(D_MODEL), const((1, D_MODEL)), const((LANES, LANES)),
                  const((LANES, LANES)), pl.BlockSpec(memory_space=pl.ANY)],
        out_specs=row(D_MODEL),
        out_shape=jax.ShapeDtypeStruct((n, D_MODEL), F32),
        scratch_shapes=[pltpu.VMEM((N_SLOT, D_MODEL // LANES, N_PICK, LANES), I32), pltpu.VMEM((2, N_PICK, LANES), F32),
                        pltpu.VMEM((GROUP, LANES), F32), pltpu.VMEM((N_PICK, LANES), F32),
                        pltpu.VMEM((tt, D_MODEL), F32), pltpu.SemaphoreType.DMA((N_SLOT,))],
        compiler_params=pltpu.CompilerParams(dimension_semantics=("arbitrary",), vmem_limit_bytes=VMEM_LIMIT),
        name="peer_ffn",
    )(eidx.reshape(-1), gate, xn2, h2, fg, jnp.ones((LANES, LANES), BF16), jnp.eye(LANES, dtype=F32), uv)


def _rope_tables(pos):
    inv_freq = ROPE_THETA ** (-jnp.arange(0, ROT_DIM, 2, dtype=F32) / ROT_DIM)
    ang = pos.astype(F32)[:, None] * inv_freq[None, :]
    cos, sin = jnp.cos(ang), jnp.sin(ang)
    n = pos.shape[0]
    rest = HEAD_DIM - ROT_DIM
    hr = ROT_DIM // 2
    c = jnp.concatenate([cos, cos, jnp.ones((n, rest), F32)], axis=1)
    a = jnp.concatenate([-sin, jnp.zeros((n, hr + rest), F32)], axis=1)
    b = jnp.concatenate([jnp.zeros((n, hr), F32), sin, jnp.zeros((n, rest), F32)], axis=1)
    return tuple(jnp.tile(z, (1, LANES // HEAD_DIM)) for z in (c, a, b))


Q_PERM = np.concatenate([np.r_[j * HEAD_DIM:(j + 1) * HEAD_DIM, (4 + j) * HEAD_DIM:(5 + j) * HEAD_DIM]
                         for j in range(4)])


def _pack_uv(u, v):
    bits = lambda a: lax.bitcast_convert_type(a.astype(jnp.bfloat16), jnp.uint16).astype(jnp.uint32)
    return lax.bitcast_convert_type((bits(u) << 16) | bits(v), I32).reshape(-1, D_MODEL // LANES, LANES)


def _prep_weights(norm_mix_g, w_in, b_fox_f, w_out, norm_mem_g, w_mem_q, w_mem_o, norm_ffn_g, peer_w_q,
                  peer_keys1, peer_keys2, peer_u, peer_v):
    sp = np.cumsum([0, 512, 128, 128, 256, 64, 4, 512, 512, 512, 8])
    a_q, a_k, a_v, a_qi, a_ki, a_w, f_q, f_k, f_v, f_f = [w_in[:, sp[i]:sp[i + 1]] for i in range(10)]
    zpad = jnp.zeros((D_MODEL, LANES - IDX_HEADS - FOX_HEADS), F32)
    w_cat = jnp.concatenate([a_q[:, Q_PERM], a_k, a_v, a_qi, a_ki, a_ki, a_w, f_f, zpad, f_q, f_k, f_v],
                            axis=1).astype(BF16)
    b_misc = jnp.concatenate([jnp.zeros((IDX_HEADS,), F32), b_fox_f.astype(F32),
                              jnp.zeros((LANES - IDX_HEADS - FOX_HEADS,), F32)])[None, :]
    r = lambda g: g.astype(F32)[None, :]
    return dict(
        g_mix=r(norm_mix_g), w_cat=w_cat, b_misc=b_misc,
        woa=w_out[:512][Q_PERM].astype(BF16), wof=w_out[512:].astype(BF16),
        g_mem=r(norm_mem_g), wq=w_mem_q.astype(BF16), wo=w_mem_o.astype(BF16), g_ffn=r(norm_ffn_g),
        wq_t=peer_w_q.T.astype(BF16), keys1=peer_keys1.astype(BF16), keys2=peer_keys2.astype(BF16),
        uv=_pack_uv(peer_u, peer_v),
    )


def _round_up(x, m):
    return (x + m - 1) // m * m


def _layer(x, q_off, past, mk_b, mv_b, w, fg, tri, cfg):
    batch, t, _ = x.shape
    n = batch * t
    x2 = x.reshape(n, D_MODEL)
    tabs = _rope_tables(q_off + jnp.arange(t, dtype=I32))
    tb_in = cfg["tb_in"]
    if tb_in > t:
        tabs = tuple(jnp.tile(z, (tb_in // t, 1)) for z in tabs)
    (aq, ak, av, qi, ki, misc, fq, fk, fv, akb, avb, kib, fkb, fvb) = _inproj(
        x2, w["g_mix"], w["w_cat"], w["b_misc"], *tabs, tb_in)
    logf = misc[:, IDX_HEADS:IDX_HEADS + FOX_HEADS].reshape(batch, t, FOX_HEADS)
    new_rows = (ak.reshape(batch, t, DSA_KV_HEADS, HEAD_DIM), av.reshape(batch, t, DSA_KV_HEADS, HEAD_DIM),
                ki[:, :IDX_DIM].reshape(batch, t, IDX_DIM), fk.reshape(batch, t, FOX_HEADS, HEAD_DIM),
                fv.reshape(batch, t, FOX_HEADS, HEAD_DIM), logf)

    kb = cfg["kb"]
    n_past = 0 if past is None else past[0].shape[1]
    n_keys = n_past + t
    lp = _round_up(n_keys, kb)

    def keys(new_b, old, width):
        new_b = new_b.reshape(batch, t, width)
        parts = [new_b] if old is None else [old.reshape(batch, n_past, -1).astype(BF16), new_b]
        if lp > n_keys:
            parts.append(jnp.zeros((batch, lp - n_keys, width), BF16))
        return parts[0] if len(parts) == 1 else jnp.concatenate(parts, axis=1)

    if past is None:
        k_all, v_all, ki_all = keys(akb, None, LANES), keys(avb, None, LANES), keys(kib, None, LANES)
        fk_all, fv_all = keys(fkb, None, 512), keys(fvb, None, 512)
        logf_all = logf
    else:
        p_k, p_v, p_ki, p_fk, p_fv, p_logf = past
        ki_dup = jnp.concatenate([p_ki, p_ki], axis=-1)
        k_all, v_all, ki_all = keys(akb, p_k, LANES), keys(avb, p_v, LANES), keys(kib, ki_dup, LANES)
        fk_all, fv_all = keys(fkb, p_fk, 512), keys(fvb, p_fv, 512)
        logf_all = jnp.concatenate([p_logf.astype(F32), logf], axis=1)

    lf = jnp.pad(logf_all, ((0, 0), (0, lp - n_keys), (0, 0)))
    lf_t = lf.reshape(batch, lp // LANES, LANES, FOX_HEADS).transpose(0, 1, 3, 2)
    d_t = _cumsum(lf_t, tri).transpose(0, 2, 1, 3).reshape(batch, FOX_HEADS, lp)
    npair = FOX_HEADS // 2
    dq_p = d_t[:, :, n_keys - t:n_keys].reshape(batch, npair, 2, t).transpose(0, 1, 3, 2)
    dk_p = d_t.reshape(batch, npair, 2, lp // kb, kb).transpose(0, 1, 3, 2, 4)

    topk = min(TOPK_MAX, n_keys // 4)
    oa = _dsa(aq, qi, misc, k_all, v_all, ki_all, tri, batch=batch, t=t, tq=cfg["tq_dsa"], kb=kb,
              n_keys=n_keys, q_off=q_off, topk=topk)
    of = _fox(fq, fk_all, fv_all, dq_p, dk_p, batch=batch, t=t, tq=cfg["tq_fox"], kb=kb,
              n_keys=n_keys, q_off=q_off)
    h2, xn2 = _post(x2, oa, of, w["woa"], w["wof"], w["g_mem"], w["wq"], mk_b, mv_b, w["wo"], w["g_ffn"],
                    batch=batch, t=t, tb=cfg["tb_post"])
    eidx, gate = _peer_sel(xn2, w["wq_t"], w["keys1"], w["keys2"], cfg["tb_sel"])
    y = _peer_ffn(eidx, gate, xn2, h2, fg, w["uv"], cfg["tt_ffn"])
    return y.reshape(batch, t, D_MODEL), new_rows


def _config(t, n_tokens, has_past):
    if has_past:
        return dict(tb_in=n_tokens, kb=384, tq_dsa=t, tq_fox=t, tb_post=t, tb_sel=n_tokens, tt_ffn=min(128, n_tokens))
    return dict(tb_in=min(512, t), kb=min(1024, t), tq_dsa=min(128, t), tq_fox=min(1024, t), tb_post=min(256, t),
                tb_sel=min(512, n_tokens), tt_ffn=min(256, n_tokens))


def kernel(x_prompt, x_sample, cache_dsa_k, cache_dsa_v, cache_dsa_kidx, cache_fox_k, cache_fox_v, cache_fox_logf, cache_mem_k, cache_mem_v, mem_prompt, norm_mix_g, w_in, b_fox_f, w_out, mem_norm_g, w_mem_kv, norm_mem_g, w_mem_q, w_mem_o, norm_ffn_g, peer_w_q, peer_keys1, peer_keys2, peer_u, peer_v, final_norm_g):
    depth = w_in.shape[0]
    assert depth == 1, "single-layer trunk"
    l = 0
    bp, tp, _ = x_prompt.shape
    bs, ts, _ = x_sample.shape
    past_len = cache_dsa_k.shape[2]
    w = _prep_weights(norm_mix_g[l], w_in[l], b_fox_f[l], w_out[l], norm_mem_g[l], w_mem_q[l], w_mem_o[l],
                      norm_ffn_g[l], peer_w_q[l], peer_keys1[l], peer_keys2[l], peer_u[l], peer_v[l])
    fg = final_norm_g.astype(F32)[None, :]
    ii = np.arange(LANES)
    tri = jnp.asarray(ii[:, None] <= ii[None, :], BF16)

    mk, mv, mkb, mvb = _memkv(mem_prompt.reshape(bp * N_MEM, D_MODEL), mem_norm_g[l].astype(F32)[None, :],
                              w_mem_kv[l].astype(BF16))
    y_p, new_p = _layer(x_prompt, 0, None, mkb.reshape(bp, N_MEM, D_MODEL), mvb.reshape(bp, N_MEM, D_MODEL),
                        w, fg, tri, _config(tp, bp * tp, False))
    past = (cache_dsa_k[l], cache_dsa_v[l], cache_dsa_kidx[l], cache_fox_k[l], cache_fox_v[l], cache_fox_logf[l])
    y_s, new_s = _layer(x_sample, past_len, past, cache_mem_k[l].reshape(bs, N_MEM, D_MODEL).astype(BF16),
                        cache_mem_v[l].reshape(bs, N_MEM, D_MODEL).astype(BF16), w, fg, tri,
                        _config(ts, bs * ts, True))
    st = lambda a: a[None]
    mem_shape = (bp, N_MEM, MEM_HEADS, MEM_HEAD_DIM)
    return (y_p, y_s) + tuple(st(a) for a in new_p) + (st(mk.reshape(mem_shape)), st(mv.reshape(mem_shape))) \
        + tuple(st(a) for a in new_s)
```
